```python
import math
import jax, jax.numpy as jnp
from jax import lax
import numpy as np

D_MODEL = 1024
BATCH = 2
SEQ = 16384
DEPTH = 2

CHUNK = 64
Q_BLOCK = 128
HEAD_DIM = 64
A_HEADS = 4
A_VDIM = 2 * HEAD_DIM
B_HEADS = 8
B_LEFT_CHUNKS = 8
B_BAND = (B_LEFT_CHUNKS + 1) * CHUNK
B_REL_CLIP = 128
C_HEADS = 8
T5_BUCKETS = 32
T5_MAX_DIST = 2048

N_BRANCH = 3
A_QK_W = A_HEADS * 2 * HEAD_DIM
A_V_W = A_HEADS * A_VDIM
B_W = B_HEADS * HEAD_DIM
C_W = C_HEADS * HEAD_DIM
IN_WIDTH = 2 * A_QK_W + A_V_W + 3 * B_W + 3 * C_W + C_HEADS + N_BRANCH * D_MODEL
D_FF = -(-8 * D_MODEL // (3 * 256)) * 256
RMS_EPS = 1e-6
NEG_INF = -1e30

kernel_name = "hybrid_gated_diff_band_forget_block"


def _split_points():
    widths = [A_QK_W, A_QK_W, A_V_W, B_W, B_W, B_W, C_W, C_W, C_W, C_HEADS, N_BRANCH * D_MODEL]
    pts, acc = [], 0
    for w in widths[:-1]:
        acc += w
        pts.append(acc)
    return pts


def _rmsnorm(x, g):
    xf = x.astype(jnp.float32)
    y = xf * lax.rsqrt(jnp.mean(xf * xf, axis=-1, keepdims=True) + RMS_EPS)
    return (y * g.astype(jnp.float32)).astype(x.dtype)


def _t5_bucket(rel):
    nb = T5_BUCKETS // 2
    max_exact = nb // 2
    n = jnp.abs(rel)
    nf = jnp.maximum(n, 1).astype(jnp.float32)
    large = max_exact + (jnp.log(nf / max_exact) / math.log(T5_MAX_DIST / max_exact) * (nb - max_exact)).astype(jnp.int32)
    large = jnp.minimum(large, nb - 1)
    return jnp.where(rel > 0, nb, 0) + jnp.where(n < max_exact, n, large)


def _diff_attention(q, k, v, t5_table, lam, lam_init, subln_g):
    b_, s_, h_ = q.shape[0], q.shape[1], q.shape[2]
    n_blk = s_ // Q_BLOCK
    scale = HEAD_DIM ** -0.5
    k_pos = jnp.arange(s_)
    k_chunk = k_pos // CHUNK
    qb = q.reshape(b_, n_blk, Q_BLOCK, h_, 2, HEAD_DIM).swapaxes(0, 1)

    def step(args):
        q_blk, i = args
        q_pos = i * Q_BLOCK + jnp.arange(Q_BLOCK)
        allowed = k_chunk[None, :] <= (q_pos // CHUNK)[:, None]
        bias = jnp.moveaxis(t5_table[_t5_bucket(k_pos[None, :] - q_pos[:, None])], -1, 0)
        s = jnp.einsum("bqhmd,bkhmd->bhmqk", q_blk, k).astype(jnp.float32) * scale
        s = jnp.where(allowed, s + bias.astype(jnp.float32)[None, :, None], NEG_INF)
        p = jax.nn.softmax(s, axis=-1)
        a = p[:, :, 0] - lam * p[:, :, 1]
        return jnp.einsum("bhqk,bkhe->bqhe", a.astype(v.dtype), v)

    o = lax.map(step, (qb, jnp.arange(n_blk)))
    o = o.swapaxes(0, 1).reshape(b_, s_, h_, A_VDIM)
    o = _rmsnorm(o, subln_g) * (1.0 - lam_init)
    return o.reshape(b_, s_, h_ * A_VDIM)


def _chunk_band_attention(q, k, v, rel_table):
    b_, s_, h_, d_ = q.shape
    n_chunk = s_ // CHUNK
    pad = B_LEFT_CHUNKS * CHUNK
    scale = d_ ** -0.5
    kp = jnp.pad(k, ((0, 0), (pad, 0), (0, 0), (0, 0)))
    vp = jnp.pad(v, ((0, 0), (pad, 0), (0, 0), (0, 0)))
    qi = jnp.arange(CHUNK)
    kj = jnp.arange(B_BAND)
    rel = jnp.clip((kj[None, :] - pad) - qi[:, None], -B_REL_CLIP, CHUNK - 1) + B_REL_CLIP
    bias = jnp.moveaxis(rel_table[rel], -1, 0).astype(jnp.float32)
    qc = q.reshape(b_, n_chunk, CHUNK, h_, d_).swapaxes(0, 1)

    def step(args):
        q_c, c = args
        start = c * CHUNK
        k_band = lax.dynamic_slice_in_dim(kp, start, B_BAND, axis=1)
        v_band = lax.dynamic_slice_in_dim(vp, start, B_BAND, axis=1)
        valid = (start - pad + kj) >= 0
        s = jnp.einsum("bqhd,bkhd->bhqk", q_c, k_band).astype(jnp.float32) * scale
        s = jnp.where(valid, s + bias, NEG_INF)
        p = jax.nn.softmax(s, axis=-1)
        return jnp.einsum("bhqk,bkhd->bqhd", p.astype(v.dtype), v_band)

    o = lax.map(step, (qc, jnp.arange(n_chunk)))
    return o.swapaxes(0, 1).reshape(b_, s_, h_ * d_)


def _forgetting_attention(q, k, v, log_f):
    b_, s_, h_, d_ = q.shape
    n_blk = s_ // Q_BLOCK
    scale = d_ ** -0.5
    c = jnp.cumsum(log_f, axis=1)
    c_k = c.transpose(0, 2, 1)
    k_pos = jnp.arange(s_)
    qb = q.reshape(b_, n_blk, Q_BLOCK, h_, d_).swapaxes(0, 1)
    cb = c.reshape(b_, n_blk, Q_BLOCK, h_).swapaxes(0, 1)

    def step(args):
        q_blk, c_blk, i = args
        q_pos = i * Q_BLOCK + jnp.arange(Q_BLOCK)
        causal = k_pos[None, :] <= q_pos[:, None]
        s = jnp.einsum("bqhd,bkhd->bhqk", q_blk, k).astype(jnp.float32) * scale
        s = s + c_blk.transpose(0, 2, 1)[..., None] - c_k[:, :, None, :]
        s = jnp.where(causal, s, NEG_INF)
        p = jax.nn.softmax(s, axis=-1)
        return jnp.einsum("bhqk,bkhd->bqhd", p.astype(v.dtype), v)

    o = lax.map(step, (qb, cb, jnp.arange(n_blk)))
    return o.swapaxes(0, 1).reshape(b_, s_, h_ * d_)


def setup_inputs(seed: int = 0) -> dict:
    key = jax.random.key(seed)
    ks = jax.random.split(key, 20)
    f32 = jnp.float32
    nrm = lambda k, shape, s: jax.random.normal(k, shape, f32) * s
    return {
        "x": nrm(ks[0], (BATCH, SEQ, D_MODEL), 1.0),
        "norm_mix_g": 1.0 + nrm(ks[1], (DEPTH, D_MODEL), 0.05),
        "w_in": nrm(ks[2], (DEPTH, D_MODEL, IN_WIDTH), D_MODEL ** -0.5),
        "b_forget": jax.random.uniform(ks[3], (DEPTH, C_HEADS), f32, 2.0, 6.0),
        "diff_lambda": nrm(ks[4], (DEPTH, 4, HEAD_DIM), 0.1),
        "diff_subln_g": 1.0 + nrm(ks[5], (DEPTH, A_VDIM), 0.05),
        "t5_table": nrm(ks[6], (T5_BUCKETS, A_HEADS), 0.5),
        "b_rel_table": nrm(ks[7], (DEPTH, B_REL_CLIP + CHUNK, B_HEADS), 0.5),
        "w_br_a": nrm(ks[8], (DEPTH, A_V_W, D_MODEL), A_V_W ** -0.5),
        "w_br_b": nrm(ks[9], (DEPTH, B_W, D_MODEL), B_W ** -0.5),
        "w_br_c": nrm(ks[10], (DEPTH, C_W, D_MODEL), C_W ** -0.5),
        "w_out": nrm(ks[11], (DEPTH, D_MODEL, D_MODEL), D_MODEL ** -0.5),
        "norm_ffn_g": 1.0 + nrm(ks[12], (DEPTH, D_MODEL), 0.05),
        "w_gate_up": nrm(ks[13], (DEPTH, D_MODEL, 2 * D_FF), D_MODEL ** -0.5),
        "w_down": nrm(ks[14], (DEPTH, D_FF, D_MODEL), D_FF ** -0.5),
        "final_norm_g": 1.0 + nrm(ks[15], (D_MODEL,), 0.05),
    }


def reference(x, norm_mix_g, w_in, b_forget, diff_lambda, diff_subln_g, t5_table, b_rel_table,
              w_br_a, w_br_b, w_br_c, w_out, norm_ffn_g, w_gate_up, w_down, final_norm_g):
    b_, s_, _ = x.shape
    splits = _split_points()
    for l in range(DEPTH):
        h = _rmsnorm(x, norm_mix_g[l])
        proj = h @ w_in[l]
        a_q, a_k, a_v, b_q, b_k, b_v, c_q, c_k, c_v, c_f, gates = jnp.split(proj, splits, axis=-1)

        lam_init = 0.8 - 0.6 * math.exp(-0.3 * l)
        lp = diff_lambda[l].astype(jnp.float32)
        lam = jnp.exp(jnp.sum(lp[0] * lp[1])) - jnp.exp(jnp.sum(lp[2] * lp[3])) + lam_init
        o_a = _diff_attention(a_q.reshape(b_, s_, A_HEADS, 2, HEAD_DIM),
                              a_k.reshape(b_, s_, A_HEADS, 2, HEAD_DIM),
                              a_v.reshape(b_, s_, A_HEADS, A_VDIM),
                              t5_table, lam, lam_init, diff_subln_g[l])

        o_b = _chunk_band_attention(b_q.reshape(b_, s_, B_HEADS, HEAD_DIM),
                                    b_k.reshape(b_, s_, B_HEADS, HEAD_DIM),
                                    b_v.reshape(b_, s_, B_HEADS, HEAD_DIM),
                                    b_rel_table[l])

        log_f = jax.nn.log_sigmoid((c_f + b_forget[l]).astype(jnp.float32))
        o_c = _forgetting_attention(c_q.reshape(b_, s_, C_HEADS, HEAD_DIM),
                                    c_k.reshape(b_, s_, C_HEADS, HEAD_DIM),
                                    c_v.reshape(b_, s_, C_HEADS, HEAD_DIM),
                                    log_f)

        g = jax.nn.sigmoid(gates.astype(jnp.float32)).astype(x.dtype).reshape(b_, s_, N_BRANCH, D_MODEL)
        merged = (g[:, :, 0] * (o_a @ w_br_a[l])
                  + g[:, :, 1] * (o_b @ w_br_b[l])
                  + g[:, :, 2] * (o_c @ w_br_c[l]))
        x = x + merged @ w_out[l]

        h = _rmsnorm(x, norm_ffn_g[l])
        gt, up = jnp.split(h @ w_gate_up[l], 2, axis=-1)
        x = x + (jax.nn.silu(gt) * up) @ w_down[l]
    return _rmsnorm(x, final_norm_g)
```

```python
import functools
import math

import jax
import jax.numpy as jnp
from jax import lax
from jax.experimental import pallas as pl
from jax.experimental.pallas import tpu as pltpu

CHUNK = 64
HEAD_DIM = 64
A_HEADS = 4
A_VDIM = 2 * HEAD_DIM
B_HEADS = 8
B_LEFT_CHUNKS = 8
B_REL_CLIP = 128
C_HEADS = 8
T5_BUCKETS = 32
T5_MAX_DIST = 2048
N_BRANCH = 3
RMS_EPS = 1e-6
NEG_INF = -1e30

LANE = 128
T = 512
A_NEAR = 3
SCAN_W = 256
VMEM_LIMIT = 56 * 1024 * 1024

F32 = jnp.float32
BF16 = jnp.bfloat16


def _cparams(*sem):
    return pltpu.CompilerParams(dimension_semantics=sem, vmem_limit_bytes=VMEM_LIMIT)


def _resident(block_shape, index_map):
    return pl.BlockSpec(block_shape, index_map, pipeline_mode=pl.Buffered(1))


def _rms(x, g):
    y = x * lax.rsqrt(jnp.mean(x * x, axis=-1, keepdims=True) + RMS_EPS)
    return y * g


def _in_proj_kernel(x_ref, g_ref, w_ref, brow_ref, wf_ref, o_ref, cf_ref, h_ref):
    @pl.when(pl.program_id(1) == 0)
    def _():
        h = _rms(x_ref[...], g_ref[...]).astype(BF16)
        h_ref[...] = h
        cf_ref[...] = jnp.dot(h, wf_ref[...], preferred_element_type=F32)

    acc = jnp.dot(h_ref[...], w_ref[...], preferred_element_type=F32)
    o_ref[...] = (acc + brow_ref[...]).astype(BF16)


def _in_proj(x2, g, w_pad, brow, wf, *, tm, tn):
    n, d = x2.shape
    ncol = w_pad.shape[1]
    return pl.pallas_call(
        _in_proj_kernel,
        out_shape=(jax.ShapeDtypeStruct((n, ncol), BF16), jax.ShapeDtypeStruct((n, LANE), F32)),
        grid=(n // tm, ncol // tn),
        in_specs=[
            pl.BlockSpec((tm, d), lambda i, j: (i, 0)),
            pl.BlockSpec((1, d), lambda i, j: (0, 0)),
            pl.BlockSpec((d, tn), lambda i, j: (0, j)),
            pl.BlockSpec((1, tn), lambda i, j: (0, j)),
            pl.BlockSpec((d, LANE), lambda i, j: (0, 0)),
        ],
        out_specs=(
            pl.BlockSpec((tm, tn), lambda i, j: (i, j)),
            pl.BlockSpec((tm, LANE), lambda i, j: (i, 0)),
        ),
        scratch_shapes=[pltpu.VMEM((tm, d), BF16)],
        compiler_params=_cparams("parallel", "arbitrary"),
        name="in_proj",
    )(x2, g, w_pad, brow, wf)


def _forget_scan_kernel(cf_ref, bf_ref, tri_ref, c_ref, carry_ref):
    @pl.when(pl.program_id(0) == 0)
    def _():
        carry_ref[...] = jnp.zeros_like(carry_ref)

    x = cf_ref[...] + bf_ref[...]
    log_f = -(jnp.maximum(-x, 0.0) + jnp.log1p(jnp.exp(-jnp.abs(x))))
    carry = carry_ref[...]
    for c in range(cf_ref.shape[1] // SCAN_W):
        blk = log_f[:, c * SCAN_W:(c + 1) * SCAN_W]
        cs = jnp.dot(blk, tri_ref[...], precision=lax.Precision.HIGHEST,
                     preferred_element_type=F32) + carry
        c_ref[:, c * SCAN_W:(c + 1) * SCAN_W] = cs
        carry = cs[:, SCAN_W - 1:SCAN_W]
    carry_ref[...] = carry


def _forget_scan(cf_rows, bf_rows, *, tw):
    r, s = cf_rows.shape
    tri = (jnp.arange(SCAN_W)[:, None] <= jnp.arange(SCAN_W)[None, :]).astype(F32)
    return pl.pallas_call(
        _forget_scan_kernel,
        out_shape=jax.ShapeDtypeStruct((r, s), F32),
        grid=(s // tw,),
        in_specs=[
            pl.BlockSpec((r, tw), lambda i: (0, i)),
            pl.BlockSpec((r, 1), lambda i: (0, 0)),
            pl.BlockSpec((SCAN_W, SCAN_W), lambda i: (0, 0)),
        ],
        out_specs=pl.BlockSpec((r, tw), lambda i: (0, i)),
        scratch_shapes=[pltpu.VMEM((r, 1), F32)],
        compiler_params=_cparams("arbitrary"),
        name="forget_scan",
    )(cf_rows, bf_rows, tri)


def _qk(q, k):
    return lax.dot_general(q, k, (((1,), (1,)), ((), ())), preferred_element_type=F32)


def _online_softmax_step(s, v, m_ref, acc_ref):
    m_prev = m_ref[...]
    m_new = jnp.maximum(m_prev, jnp.max(s, axis=-1, keepdims=True))
    alpha = jnp.exp(m_prev - m_new)
    p = jnp.exp(s - m_new)
    acc_ref[...] = acc_ref[...] * alpha + jnp.dot(p.astype(BF16), v, preferred_element_type=F32)
    m_ref[...] = m_new


def _kv_rows(j):
    return pl.ds(pl.multiple_of(j * T, T), T)


def _attn_a_kernel(q_ref, k_ref, v_ref, tile_ref, far_ref, lp_ref, sg_ref, o_ref,
                   m0_ref, m1_ref, acc0_ref, acc1_ref, *, lam_init):
    i = pl.program_id(2)
    for m_ref, acc_ref in ((m0_ref, acc0_ref), (m1_ref, acc1_ref)):
        m_ref[...] = jnp.full_like(m_ref, NEG_INF)
        acc_ref[...] = jnp.zeros_like(acc_ref)
    q = q_ref[...]

    def step(j, bias):
        rows = _kv_rows(j)
        k = k_ref[rows, :]
        v = v_ref[rows, :]
        for sub, (m_ref, acc_ref) in enumerate(((m0_ref, acc0_ref), (m1_ref, acc1_ref))):
            lanes = slice(sub * LANE, (sub + 1) * LANE)
            s = _qk(q[:, lanes], k[:, lanes]) + bias
            _online_softmax_step(s, v, m_ref, acc_ref)

    def far_body(j, carry):
        step(j, far_ref[0])
        return carry

    lax.fori_loop(0, jnp.maximum(i - (A_NEAR - 1), 0), far_body, 0)
    for d in range(A_NEAR - 1, 0, -1):
        @pl.when(i >= d)
        def _(d=d):
            step(i - d, tile_ref[d, 0])
    step(i, tile_ref[0, 0])

    lp = lp_ref[...]
    lam = (jnp.exp(jnp.sum(lp[0:1] * lp[1:2], axis=-1, keepdims=True))
           - jnp.exp(jnp.sum(lp[2:3] * lp[3:4], axis=-1, keepdims=True)) + lam_init)
    acc0 = acc0_ref[...]
    acc1 = acc1_ref[...]
    o = (acc0[:, :A_VDIM] / acc0[:, A_VDIM:A_VDIM + 1]
         - lam * (acc1[:, :A_VDIM] / acc1[:, A_VDIM:A_VDIM + 1]))
    o_ref[...] = (_rms(o, sg_ref[...]) * (1.0 - lam_init)).astype(BF16)


def _attn_a(proj, tiles, far, lp, sg, *, b, s, lam_init):
    nq = s // T
    wa = 2 * LANE
    return pl.pallas_call(
        functools.partial(_attn_a_kernel, lam_init=lam_init),
        out_shape=jax.ShapeDtypeStruct((b * s, A_HEADS * A_VDIM), BF16),
        grid=(b, A_HEADS, nq),
        in_specs=[
            pl.BlockSpec((T, wa), lambda bi, h, i: (bi * nq + i, h)),
            _resident((s, wa), lambda bi, h, i: (bi, A_HEADS + h)),
            _resident((s, wa), lambda bi, h, i: (bi, 2 * A_HEADS + h)),
            _resident((A_NEAR, 1, T, T), lambda bi, h, i: (0, h, 0, 0)),
            pl.BlockSpec((1, 1, 1), lambda bi, h, i: (h, 0, 0)),
            pl.BlockSpec((4, HEAD_DIM), lambda bi, h, i: (0, 0)),
            pl.BlockSpec((1, A_VDIM), lambda bi, h, i: (0, 0)),
        ],
        out_specs=pl.BlockSpec((T, A_VDIM), lambda bi, h, i: (bi * nq + i, h)),
        scratch_shapes=[pltpu.VMEM((T, 1), F32), pltpu.VMEM((T, 1), F32),
                        pltpu.VMEM((T, wa), F32), pltpu.VMEM((T, wa), F32)],
        compiler_params=_cparams("parallel", "parallel", "arbitrary"),
        name="attn_a",
    )(proj, proj, proj, tiles, far, lp, sg)


def _normalized(acc_ref):
    acc = acc_ref[...]
    return (acc / acc[:, HEAD_DIM:HEAD_DIM + 1]).astype(BF16)


def _attn_b_kernel(q_ref, k_ref, v_ref, tile_ref, o_ref, m_ref, acc_ref):
    i = pl.program_id(2)
    m_ref[...] = jnp.full_like(m_ref, NEG_INF)
    acc_ref[...] = jnp.zeros_like(acc_ref)
    q = q_ref[...]

    def step(j, bias):
        rows = _kv_rows(j)
        _online_softmax_step(_qk(q, k_ref[rows, :]) + bias, v_ref[rows, :], m_ref, acc_ref)

    @pl.when(i >= 1)
    def _():
        step(i - 1, tile_ref[1, 0])

    step(i, tile_ref[0, 0])
    o_ref[...] = _normalized(acc_ref)


def _attn_c_kernel(q_ref, k_ref, v_ref, ccol_ref, crow_ref, o_ref, m_ref, acc_ref):
    i = pl.program_id(2)
    m_ref[...] = jnp.full_like(m_ref, NEG_INF)
    acc_ref[...] = jnp.zeros_like(acc_ref)
    q = q_ref[...]
    c_q = ccol_ref[0, 0]

    def logits(j):
        rows = _kv_rows(j)
        c_k = crow_ref[0, 0, :, rows]
        return _qk(q, k_ref[rows, :]) + c_q - c_k, v_ref[rows, :]

    def body(j, carry):
        s, v = logits(j)
        _online_softmax_step(s, v, m_ref, acc_ref)
        return carry

    lax.fori_loop(0, i, body, 0)
    s, v = logits(i)
    causal = (lax.broadcasted_iota(jnp.int32, (T, T), 1)
              <= lax.broadcasted_iota(jnp.int32, (T, T), 0))
    _online_softmax_step(jnp.where(causal, s, NEG_INF), v, m_ref, acc_ref)
    o_ref[...] = _normalized(acc_ref)


def _attn_bc(kernel, proj, extra, extra_specs, *, b, s, heads, slot0, name):
    nq = s // T
    return pl.pallas_call(
        kernel,
        out_shape=jax.ShapeDtypeStruct((b * s, heads * LANE), BF16),
        grid=(b, heads, nq),
        in_specs=[
            pl.BlockSpec((T, LANE), lambda bi, h, i: (bi * nq + i, slot0 + h)),
            _resident((s, LANE), lambda bi, h, i: (bi, slot0 + heads + h)),
            _resident((s, LANE), lambda bi, h, i: (bi, slot0 + 2 * heads + h)),
            *extra_specs,
        ],
        out_specs=pl.BlockSpec((T, LANE), lambda bi, h, i: (bi * nq + i, h)),
        scratch_shapes=[pltpu.VMEM((T, 1), F32), pltpu.VMEM((T, LANE), F32)],
        compiler_params=_cparams("parallel", "parallel", "arbitrary"),
        name=name,
    )(proj, proj, proj, *extra)


def _merge_kernel(x_ref, g_ref, wg_ref, oa_ref, ob_ref, oc_ref, wa_ref, wb_ref, wc_ref,
                  wo_ref, o_ref):
    x = x_ref[...]
    d = x.shape[1]
    h = _rms(x, g_ref[...]).astype(BF16)
    gates = jax.nn.sigmoid(jnp.dot(h, wg_ref[...], preferred_element_type=F32))
    merged = None
    for n, (o_br, w_br) in enumerate(((oa_ref, wa_ref), (ob_ref, wb_ref), (oc_ref, wc_ref))):
        term = gates[:, n * d:(n + 1) * d] * jnp.dot(o_br[...], w_br[...],
                                                     preferred_element_type=F32)
        merged = term if merged is None else merged + term
    o_ref[...] = x + jnp.dot(merged.astype(BF16), wo_ref[...], preferred_element_type=F32)


def _merge(x2, g, wg, oa, ob, oc, wa, wb, wc, wo, *, tm):
    n, d = x2.shape
    row = lambda i: (i, 0)
    const = lambda i: (0, 0)
    return pl.pallas_call(
        _merge_kernel,
        out_shape=jax.ShapeDtypeStruct((n, d), F32),
        grid=(n // tm,),
        in_specs=[
            pl.BlockSpec((tm, d), row),
            pl.BlockSpec((1, d), const),
            _resident(wg.shape, const),
            pl.BlockSpec((tm, oa.shape[1]), row),
            pl.BlockSpec((tm, ob.shape[1]), row),
            pl.BlockSpec((tm, oc.shape[1]), row),
            _resident(wa.shape, const),
            _resident(wb.shape, const),
            _resident(wc.shape, const),
            _resident(wo.shape, const),
        ],
        out_specs=pl.BlockSpec((tm, d), row),
        compiler_params=_cparams("parallel"),
        name="merge",
    )(x2, g, wg, oa, ob, oc, wa, wb, wc, wo)


def _ffn_kernel(x_ref, g_ref, wg_ref, wu_ref, wd_ref, gf_ref, o_ref, *, n_chunks, final_norm):
    x = x_ref[...]
    h = _rms(x, g_ref[...]).astype(BF16)
    fc = wg_ref.shape[1] // n_chunks
    y = x
    for c in range(n_chunks):
        cols = slice(c * fc, (c + 1) * fc)
        gt = jnp.dot(h, wg_ref[:, cols], preferred_element_type=F32)
        up = jnp.dot(h, wu_ref[:, cols], preferred_element_type=F32)
        act = (gt * jax.nn.sigmoid(gt) * up).astype(BF16)
        y = y + jnp.dot(act, wd_ref[cols, :], preferred_element_type=F32)
    if final_norm:
        y = _rms(y, gf_ref[...])
    o_ref[...] = y


def _ffn(x2, g, wg, wu, wd, gf, *, tm, n_chunks, final_norm):
    n, d = x2.shape
    row = lambda i: (i, 0)
    const = lambda i: (0, 0)
    return pl.pallas_call(
        functools.partial(_ffn_kernel, n_chunks=n_chunks, final_norm=final_norm),
        out_shape=jax.ShapeDtypeStruct((n, d), F32),
        grid=(n // tm,),
        in_specs=[
            pl.BlockSpec((tm, d), row),
            pl.BlockSpec((1, d), const),
            _resident(wg.shape, const),
            _resident(wu.shape, const),
            _resident(wd.shape, const),
            pl.BlockSpec((1, d), const),
        ],
        out_specs=pl.BlockSpec((tm, d), row),
        compiler_params=_cparams("parallel"),
        name="ffn",
    )(x2, g, wg, wu, wd, gf)


def _pad_heads(w, heads, width, slot):
    d = w.shape[0]
    w = w.reshape(d, heads, width)
    return jnp.pad(w, ((0, 0), (0, 0), (0, slot - width))).reshape(d, heads * slot)


def _pad_head_rows(w, heads, width, slot):
    d = w.shape[1]
    w = w.reshape(heads, width, d)
    return jnp.pad(w, ((0, 0), (0, slot - width), (0, 0))).reshape(heads * slot, d)


def _ones_row(heads, slot, at):
    row = jnp.zeros((heads, slot), F32).at[:, at].set(1.0)
    return row.reshape(1, heads * slot)


def _t5_bucket(rel):
    nb = T5_BUCKETS // 2
    max_exact = nb // 2
    n = jnp.abs(rel)
    nf = jnp.maximum(n, 1).astype(jnp.float32)
    large = max_exact + (jnp.log(nf / max_exact) / math.log(T5_MAX_DIST / max_exact)
                         * (nb - max_exact)).astype(jnp.int32)
    large = jnp.minimum(large, nb - 1)
    return jnp.where(rel > 0, nb, 0) + jnp.where(n < max_exact, n, large)


def _block_rel(n_tiles):
    r = jnp.arange(T)[None, :, None]
    c = jnp.arange(T)[None, None, :]
    d = jnp.arange(n_tiles)[:, None, None]
    return c - r - d * T, r // CHUNK, c // CHUNK


def _t5_tiles(t5_table):
    rel, rq, ck = _block_rel(A_NEAR)
    bias = jnp.moveaxis(t5_table[_t5_bucket(rel)], -1, 1).astype(F32)
    allowed = (jnp.arange(A_NEAR)[:, None, None] > 0) | (ck <= rq)
    return jnp.where(allowed[:, None], bias, NEG_INF)


def _band_tiles(rel_table):
    rel, rq, ck = _block_rel(2)
    idx = jnp.clip(rel, -B_REL_CLIP, CHUNK - 1) + B_REL_CLIP
    bias = jnp.moveaxis(rel_table[idx], -1, 1).astype(F32)
    is_prev = jnp.arange(2)[:, None, None] > 0
    allowed = jnp.where(is_prev, ck >= rq, ck <= rq)
    return jnp.where(allowed[:, None], bias, NEG_INF)


def kernel(x, norm_mix_g, w_in, b_forget, diff_lambda, diff_subln_g, t5_table, b_rel_table,
           w_br_a, w_br_b, w_br_c, w_out, norm_ffn_g, w_gate_up, w_down, final_norm_g):
    b, s, d = x.shape
    depth = w_in.shape[0]
    assert s % T == 0 and T // CHUNK == B_LEFT_CHUNKS
    n = b * s
    scale = HEAD_DIM ** -0.5
    a_qk_w = A_HEADS * 2 * HEAD_DIM
    a_v_w = A_HEADS * A_VDIM
    bw = B_HEADS * HEAD_DIM
    cw = C_HEADS * HEAD_DIM
    d_ff = w_down.shape[1]
    x2 = x.reshape(n, d)

    t5_tiles = _t5_tiles(t5_table)
    t5_far = t5_table[T5_BUCKETS // 2 - 1].astype(F32).reshape(A_HEADS, 1, 1)
    brow = jnp.concatenate([
        jnp.zeros((1, 2 * 2 * A_HEADS * LANE), F32), _ones_row(A_HEADS, 2 * LANE, A_VDIM),
        jnp.zeros((1, 2 * B_HEADS * LANE), F32), _ones_row(B_HEADS, LANE, HEAD_DIM),
        jnp.zeros((1, 2 * C_HEADS * LANE), F32), _ones_row(C_HEADS, LANE, HEAD_DIM),
    ], axis=1)

    for l in range(depth):
        w = w_in[l]
        pieces, at = [], 0
        for width in (a_qk_w, a_qk_w, a_v_w, bw, bw, bw, cw, cw, cw, C_HEADS, N_BRANCH * d):
            pieces.append(w[:, at:at + width])
            at += width
        a_q, a_k, a_v, b_q, b_k, b_v, c_q, c_k, c_v, c_f, w_gates = pieces
        w_pad = jnp.concatenate([
            _pad_heads(a_q * scale, 2 * A_HEADS, HEAD_DIM, LANE),
            _pad_heads(a_k, 2 * A_HEADS, HEAD_DIM, LANE),
            _pad_heads(a_v, A_HEADS, A_VDIM, 2 * LANE),
            _pad_heads(b_q * scale, B_HEADS, HEAD_DIM, LANE),
            _pad_heads(b_k, B_HEADS, HEAD_DIM, LANE),
            _pad_heads(b_v, B_HEADS, HEAD_DIM, LANE),
            _pad_heads(c_q * scale, C_HEADS, HEAD_DIM, LANE),
            _pad_heads(c_k, C_HEADS, HEAD_DIM, LANE),
            _pad_heads(c_v, C_HEADS, HEAD_DIM, LANE),
        ], axis=1).astype(BF16)
        wf = jnp.pad(c_f, ((0, 0), (0, LANE - C_HEADS))).astype(BF16)

        proj, cf = _in_proj(x2, norm_mix_g[l][None], w_pad, brow, wf,
                            tm=min(1024, n), tn=w_pad.shape[1] // 4)

        cf_rows = cf[:, :C_HEADS].reshape(b, s, C_HEADS).transpose(0, 2, 1).reshape(b * C_HEADS, s)
        bf_rows = jnp.tile(b_forget[l].astype(F32), b).reshape(b * C_HEADS, 1)
        c_sum = _forget_scan(cf_rows, bf_rows, tw=min(2048, s))

        lam_init = 0.8 - 0.6 * math.exp(-0.3 * l)
        o_a = _attn_a(proj, t5_tiles, t5_far, diff_lambda[l].astype(F32),
                      diff_subln_g[l][None].astype(F32), b=b, s=s, lam_init=lam_init)

        o_b = _attn_bc(
            _attn_b_kernel, proj, (_band_tiles(b_rel_table[l]),),
            (_resident((2, 1, T, T), lambda bi, h, i: (0, h, 0, 0)),),
            b=b, s=s, heads=B_HEADS, slot0=6 * A_HEADS, name="attn_b")

        nq = s // T
        o_c = _attn_bc(
            _attn_c_kernel, proj,
            (c_sum.reshape(b, C_HEADS, s, 1), c_sum.reshape(b, C_HEADS, 1, s)),
            (pl.BlockSpec((1, 1, T, 1), lambda bi, h, i: (bi, h, i, 0)),
             _resident((1, 1, 1, s), lambda bi, h, i: (bi, h, 0, 0))),
            b=b, s=s, heads=C_HEADS, slot0=6 * A_HEADS + 3 * B_HEADS, name="attn_c")

        x2 = _merge(
            x2, norm_mix_g[l][None], w_gates.astype(BF16), o_a, o_b, o_c,
            w_br_a[l].astype(BF16),
            _pad_head_rows(w_br_b[l], B_HEADS, HEAD_DIM, LANE).astype(BF16),
            _pad_head_rows(w_br_c[l], C_HEADS, HEAD_DIM, LANE).astype(BF16),
            w_out[l].astype(BF16), tm=min(512, n))

        x2 = _ffn(
            x2, norm_ffn_g[l][None], w_gate_up[l][:, :d_ff].astype(BF16),
            w_gate_up[l][:, d_ff:].astype(BF16), w_down[l].astype(BF16),
            final_norm_g[None], tm=min(512, n), n_chunks=2, final_norm=(l == depth - 1))

    return x2.reshape(b, s, d)
```

```python
import functools
import math

import numpy as np
import jax
import jax.numpy as jnp
from jax import lax
from jax.experimental import pallas as pl
from jax.experimental.pallas import tpu as pltpu

CHUNK = 64
HEAD_DIM = 64
A_HEADS = 4
A_VDIM = 2 * HEAD_DIM
B_HEADS = 8
B_LEFT_CHUNKS = 8
B_REL_CLIP = 128
C_HEADS = 8
T5_BUCKETS = 32
T5_MAX_DIST = 2048
N_BRANCH = 3
RMS_EPS = 1e-6
NEG_INF = -1e30

LANE = 128
BF16_ROWS = 16
T = 512
A_NEAR = 3
VA_ROWS = A_VDIM + BF16_ROWS
VBC_ROWS = HEAD_DIM + BF16_ROWS
C_PARTS = 3
VMEM_LIMIT = 56 * 1024 * 1024

F32 = jnp.float32
BF16 = jnp.bfloat16


def _cparams(*sem):
    return pltpu.CompilerParams(dimension_semantics=sem, vmem_limit_bytes=VMEM_LIMIT)


def _resident(block_shape, index_map):
    return pl.BlockSpec(block_shape, index_map, pipeline_mode=pl.Buffered(1))


def _rms(x, g, axis=-1):
    y = x * lax.rsqrt(jnp.mean(x * x, axis=axis, keepdims=True) + RMS_EPS)
    return y * g


def _dot_nt(a, b):
    return lax.dot_general(a, b, (((1,), (1,)), ((), ())), preferred_element_type=F32)


def _in_proj_kernel(x_ref, g_ref, w_ref, brow_ref, wf_ref, wva_ref, wvb_ref, wvc_ref,
                    onea_ref, onebc_ref, o_ref, cf_ref, va_ref, vb_ref, vc_ref, h_ref):
    @pl.when(pl.program_id(1) == 0)
    def _():
        h = _rms(x_ref[...], g_ref[...]).astype(BF16)
        h_ref[...] = h
        cf_ref[...] = jnp.dot(h, wf_ref[...], preferred_element_type=F32)
        va_ref[...] = (_dot_nt(wva_ref[...], h) + onea_ref[...]).astype(BF16)
        vb_ref[...] = (_dot_nt(wvb_ref[...], h) + onebc_ref[...]).astype(BF16)
        vc_ref[...] = (_dot_nt(wvc_ref[...], h) + onebc_ref[...]).astype(BF16)

    acc = jnp.dot(h_ref[...], w_ref[...], preferred_element_type=F32)
    o_ref[...] = (acc + brow_ref[...]).astype(BF16)


def _in_proj(x2, g, w_qk, brow, wf, wva, wvb, wvc, onea, onebc, *, b, tm, tn):
    n, d = x2.shape
    s = n // b
    nt = s // tm
    ncol = w_qk.shape[1]
    ra, rbc = wva.shape[0], wvb.shape[0]
    const = lambda i, j: (0, 0)
    vt_map = lambda i, j: (i // nt, i % nt)
    return pl.pallas_call(
        _in_proj_kernel,
        out_shape=(jax.ShapeDtypeStruct((n, ncol), BF16), jax.ShapeDtypeStruct((n, LANE), F32),
                   jax.ShapeDtypeStruct((b * ra, s), BF16), jax.ShapeDtypeStruct((b * rbc, s), BF16),
                   jax.ShapeDtypeStruct((b * rbc, s), BF16)),
        grid=(n // tm, ncol // tn),
        in_specs=[
            pl.BlockSpec((tm, d), lambda i, j: (i, 0)),
            pl.BlockSpec((1, d), const),
            pl.BlockSpec((d, tn), lambda i, j: (0, j)),
            pl.BlockSpec((1, tn), lambda i, j: (0, j)),
            pl.BlockSpec((d, LANE), const),
            _resident((ra, d), const),
            _resident((rbc, d), const),
            _resident((rbc, d), const),
            pl.BlockSpec((ra, 1), const),
            pl.BlockSpec((rbc, 1), const),
        ],
        out_specs=(
            pl.BlockSpec((tm, tn), lambda i, j: (i, j)),
            pl.BlockSpec((tm, LANE), lambda i, j: (i, 0)),
            pl.BlockSpec((ra, tm), vt_map),
            pl.BlockSpec((rbc, tm), vt_map),
            pl.BlockSpec((rbc, tm), vt_map),
        ),
        scratch_shapes=[pltpu.VMEM((tm, d), BF16)],
        compiler_params=_cparams("parallel", "arbitrary"),
        name="in_proj",
    )(x2, g, w_qk, brow, wf, wva, wvb, wvc, onea, onebc)


def _augment_kernel(cf_ref, bf_ref, tri_ref, q_ref, k_ref, pq_ref, pk_ref, qrow_ref, krow_ref,
                    qo_ref, ko_ref, carry_ref):
    @pl.when(pl.program_id(1) == 0)
    def _():
        carry_ref[...] = jnp.zeros_like(carry_ref)

    x = cf_ref[...] + bf_ref[...]
    log_f = -(jnp.maximum(-x, 0.0) + jnp.log1p(jnp.exp(-jnp.abs(x))))
    c = jnp.dot(tri_ref[...], log_f, precision=lax.Precision.HIGHEST,
                preferred_element_type=F32) + carry_ref[...]
    carry_ref[...] = c[c.shape[0] - 1:, :]
    parts, rest = [], c
    for _ in range(C_PARTS):
        piece = rest.astype(BF16)
        parts.append(piece)
        rest = rest - piece.astype(F32)
    parts = jnp.concatenate(parts, axis=1)
    placed_q = jnp.dot(parts, pq_ref[...], preferred_element_type=F32)
    placed_k = jnp.dot(parts, pk_ref[...], preferred_element_type=F32)
    qo_ref[...] = (q_ref[...].astype(F32) + placed_q + qrow_ref[...]).astype(BF16)
    ko_ref[...] = (k_ref[...].astype(F32) - placed_k + krow_ref[...]).astype(BF16)


def _augment_constants():
    w = C_HEADS * LANE
    pq = np.zeros((C_PARTS * LANE, w), np.float32)
    pk = np.zeros((C_PARTS * LANE, w), np.float32)
    qrow = np.zeros((1, w), np.float32)
    krow = np.zeros((1, w), np.float32)
    for h in range(C_HEADS):
        for p in range(C_PARTS):
            pq[p * LANE + h, h * LANE + HEAD_DIM + p] = 1.0
            pk[p * LANE + h, h * LANE + HEAD_DIM + C_PARTS + p] = 1.0
            krow[0, h * LANE + HEAD_DIM + p] = 1.0
            qrow[0, h * LANE + HEAD_DIM + C_PARTS + p] = 1.0
    return jnp.asarray(pq, BF16), jnp.asarray(pk, BF16), jnp.asarray(qrow), jnp.asarray(krow)


def _augment(cf, bf_row, proj, *, b, q_block, k_block, tm):
    n = cf.shape[0]
    nt = n // b // tm
    w = C_HEADS * LANE
    tri = jnp.asarray(np.tril(np.ones((tm, tm), np.float32)))
    pq, pk, qrow, krow = _augment_constants()
    const = lambda bi, i: (0, 0)
    row = lambda bi, i: (bi * nt + i, 0)
    return pl.pallas_call(
        _augment_kernel,
        out_shape=(jax.ShapeDtypeStruct((n, w), BF16), jax.ShapeDtypeStruct((n, w), BF16)),
        grid=(b, nt),
        in_specs=[
            pl.BlockSpec((tm, LANE), row),
            pl.BlockSpec((1, LANE), const),
            pl.BlockSpec((tm, tm), const),
            pl.BlockSpec((tm, w), lambda bi, i: (bi * nt + i, q_block)),
            pl.BlockSpec((tm, w), lambda bi, i: (bi * nt + i, k_block)),
            pl.BlockSpec(pq.shape, const),
            pl.BlockSpec(pk.shape, const),
            pl.BlockSpec((1, w), const),
            pl.BlockSpec((1, w), const),
        ],
        out_specs=(pl.BlockSpec((tm, w), row), pl.BlockSpec((tm, w), row)),
        scratch_shapes=[pltpu.VMEM((1, LANE), F32)],
        compiler_params=_cparams("arbitrary", "arbitrary"),
        name="augment",
    )(cf, bf_row, tri, proj, proj, pq, pk, qrow, krow)


def _bias_tile_kernel(x_ref, o_ref, *, band):
    d = pl.program_id(0)
    x = jnp.broadcast_to(x_ref[0, 0], (T, 2 * T))
    tile = pltpu.roll(x, T + 1, 1, stride=1, stride_axis=0)[:, :T]
    ahead = (lax.broadcasted_iota(jnp.int32, (T, T), 0) // CHUNK
             - lax.broadcasted_iota(jnp.int32, (T, T), 1) // CHUNK)
    sign = jnp.where(d > 0, -1 if band else 0, 1)
    o_ref[0, 0] = jnp.where(ahead * sign <= 0, tile, NEG_INF)


def _bias_tiles(x, *, band):
    n_tiles, heads = x.shape[:2]
    return pl.pallas_call(
        functools.partial(_bias_tile_kernel, band=band),
        out_shape=jax.ShapeDtypeStruct((n_tiles, heads, T, T), F32),
        grid=(n_tiles, heads),
        in_specs=[pl.BlockSpec((1, 1, 1, 2 * T), lambda d, h: (d, h, 0, 0))],
        out_specs=pl.BlockSpec((1, 1, T, T), lambda d, h: (d, h, 0, 0)),
        compiler_params=_cparams("parallel", "parallel"),
        name="bias_tiles_band" if band else "bias_tiles_t5",
    )(x)


def _softmax_step(s, vt, m_ref, acc_ref):
    rows = vt.shape[0]
    m_prev = m_ref[...]
    m_new = jnp.maximum(m_prev, jnp.max(s, axis=0, keepdims=True))
    alpha = jnp.exp(m_prev - m_new)
    p = jnp.exp(s - m_new).astype(BF16)
    acc_ref[:rows, :] = acc_ref[:rows, :] * alpha + jnp.dot(vt, p, preferred_element_type=F32)
    m_ref[...] = m_new


def _keys(j, n_blocks=1):
    return pl.ds(pl.multiple_of(j * T, T), n_blocks * T)


def _init(m_ref, acc_ref):
    m_ref[...] = jnp.full_like(m_ref, NEG_INF)
    acc_ref[...] = jnp.zeros_like(acc_ref)


def _attn_a_kernel(q_ref, k_ref, vt_ref, tile_ref, far_ref, lp_ref, sg_ref, o_ref,
                   m0_ref, m1_ref, acc0_ref, acc1_ref, *, lam_init):
    i = pl.program_id(2)
    streams = ((m0_ref, acc0_ref), (m1_ref, acc1_ref))
    for m_ref, acc_ref in streams:
        _init(m_ref, acc_ref)
    q = q_ref[...]

    def step(j, n_blocks, bias):
        keys = _keys(j, n_blocks)
        k = k_ref[keys, :]
        vt = vt_ref[:, keys]
        for sub, (m_ref, acc_ref) in enumerate(streams):
            lanes = slice(sub * LANE, (sub + 1) * LANE)
            _softmax_step(_dot_nt(k[:, lanes], q[:, lanes]) + bias, vt, m_ref, acc_ref)

    n_far = jnp.maximum(i - (A_NEAR - 1), 0)

    def far_body(t, carry):
        step(2 * t, 2, far_ref[0])
        return carry

    lax.fori_loop(0, n_far // 2, far_body, 0)

    @pl.when(n_far % 2 == 1)
    def _():
        step(n_far - 1, 1, far_ref[0])

    for d in range(A_NEAR - 1, 0, -1):
        @pl.when(i >= d)
        def _(d=d):
            step(i - d, 1, tile_ref[d, 0])
    step(i, 1, tile_ref[0, 0])

    lp = lp_ref[...]
    lam = (jnp.exp(jnp.sum(lp[0:1] * lp[1:2], axis=-1, keepdims=True))
           - jnp.exp(jnp.sum(lp[2:3] * lp[3:4], axis=-1, keepdims=True)) + lam_init)
    acc0 = acc0_ref[...]
    acc1 = acc1_ref[...]
    ot = (acc0[:A_VDIM] / acc0[A_VDIM:A_VDIM + 1]
          - lam * (acc1[:A_VDIM] / acc1[A_VDIM:A_VDIM + 1]))
    ot = _rms(ot, sg_ref[...], axis=0) * (1.0 - lam_init)
    o_ref[...] = ot.T.astype(BF16)


def _attn_a(qk, vt, tiles, far, lp, sg, *, b, s, lam_init):
    nq = s // T
    wa = 2 * LANE
    return pl.pallas_call(
        functools.partial(_attn_a_kernel, lam_init=lam_init),
        out_shape=jax.ShapeDtypeStruct((b * s, A_HEADS * A_VDIM), BF16),
        grid=(b, A_HEADS, nq),
        in_specs=[
            pl.BlockSpec((T, wa), lambda bi, h, i: (bi * nq + i, h)),
            _resident((s, wa), lambda bi, h, i: (bi, A_HEADS + h)),
            _resident((VA_ROWS, s), lambda bi, h, i: (bi * A_HEADS + h, 0)),
            _resident((A_NEAR, 1, T, T), lambda bi, h, i: (0, h, 0, 0)),
            pl.BlockSpec((1, 1, 1), lambda bi, h, i: (h, 0, 0)),
            pl.BlockSpec((4, HEAD_DIM), lambda bi, h, i: (0, 0)),
            pl.BlockSpec((A_VDIM, 1), lambda bi, h, i: (0, 0)),
        ],
        out_specs=pl.BlockSpec((T, A_VDIM), lambda bi, h, i: (bi * nq + i, h)),
        scratch_shapes=[pltpu.VMEM((1, T), F32), pltpu.VMEM((1, T), F32),
                        pltpu.VMEM((VA_ROWS, T), F32), pltpu.VMEM((VA_ROWS, T), F32)],
        compiler_params=_cparams("parallel", "parallel", "arbitrary"),
        name="attn_a",
    )(qk, qk, vt, tiles, far, lp, sg)


def _normalized(acc_ref):
    acc = acc_ref[...]
    return (acc / acc[HEAD_DIM:HEAD_DIM + 1]).T.astype(BF16)


def _attn_b_kernel(q_ref, k_ref, vt_ref, tile_ref, o_ref, m_ref, acc_ref):
    i = pl.program_id(2)
    _init(m_ref, acc_ref)
    q = q_ref[...]

    def step(j, bias):
        keys = _keys(j)
        _softmax_step(_dot_nt(k_ref[keys, :], q) + bias, vt_ref[:, keys], m_ref, acc_ref)

    @pl.when(i >= 1)
    def _():
        step(i - 1, tile_ref[1, 0])

    step(i, tile_ref[0, 0])
    o_ref[...] = _normalized(acc_ref)


def _attn_c_kernel(q_ref, k_ref, vt_ref, o_ref, m_ref, acc_ref):
    i = pl.program_id(2)
    _init(m_ref, acc_ref)
    q = q_ref[...]

    def step(j, n_blocks, masked):
        keys = _keys(j, n_blocks)
        s = _dot_nt(k_ref[keys, :], q)
        if masked:
            causal = (lax.broadcasted_iota(jnp.int32, s.shape, 0)
                      <= lax.broadcasted_iota(jnp.int32, s.shape, 1))
            s = jnp.where(causal, s, NEG_INF)
        _softmax_step(s, vt_ref[:, keys], m_ref, acc_ref)

    def body(t, carry):
        step(2 * t, 2, False)
        return carry

    lax.fori_loop(0, i // 2, body, 0)

    @pl.when(i % 2 == 1)
    def _():
        step(i - 1, 1, False)

    step(i, 1, True)
    o_ref[...] = _normalized(acc_ref)


def _attn_bc(kernel, q_arr, k_arr, vt, extra, extra_specs, *, b, s, heads, q_slot0, k_slot0, name):
    nq = s // T
    return pl.pallas_call(
        kernel,
        out_shape=jax.ShapeDtypeStruct((b * s, heads * LANE), BF16),
        grid=(b, heads, nq),
        in_specs=[
            pl.BlockSpec((T, LANE), lambda bi, h, i: (bi * nq + i, q_slot0 + h)),
            _resident((s, LANE), lambda bi, h, i: (bi, k_slot0 + h)),
            _resident((VBC_ROWS, s), lambda bi, h, i: (bi * heads + h, 0)),
            *extra_specs,
        ],
        out_specs=pl.BlockSpec((T, LANE), lambda bi, h, i: (bi * nq + i, h)),
        scratch_shapes=[pltpu.VMEM((1, T), F32), pltpu.VMEM((LANE, T), F32)],
        compiler_params=_cparams("parallel", "parallel", "arbitrary"),
        name=name,
    )(q_arr, k_arr, vt, *extra)


def _merge_kernel(x_ref, g_ref, wg_ref, oa_ref, ob_ref, oc_ref, wa_ref, wb_ref, wc_ref,
                  wo_ref, o_ref):
    x = x_ref[...]
    d = x.shape[1]
    h = _rms(x, g_ref[...]).astype(BF16)
    gates = jax.nn.sigmoid(jnp.dot(h, wg_ref[...], preferred_element_type=F32))
    merged = None
    for n, (o_br, w_br) in enumerate(((oa_ref, wa_ref), (ob_ref, wb_ref), (oc_ref, wc_ref))):
        term = gates[:, n * d:(n + 1) * d] * jnp.dot(o_br[...], w_br[...],
                                                     preferred_element_type=F32)
        merged = term if merged is None else merged + term
    o_ref[...] = x + jnp.dot(merged.astype(BF16), wo_ref[...], preferred_element_type=F32)


def _merge(x2, g, wg, oa, ob, oc, wa, wb, wc, wo, *, tm):
    n, d = x2.shape
    row = lambda i: (i, 0)
    const = lambda i: (0, 0)
    return pl.pallas_call(
        _merge_kernel,
        out_shape=jax.ShapeDtypeStruct((n, d), F32),
        grid=(n // tm,),
        in_specs=[
            pl.BlockSpec((tm, d), row),
            pl.BlockSpec((1, d), const),
            _resident(wg.shape, const),
            pl.BlockSpec((tm, oa.shape[1]), row),
            pl.BlockSpec((tm, ob.shape[1]), row),
            pl.BlockSpec((tm, oc.shape[1]), row),
            _resident(wa.shape, const),
            _resident(wb.shape, const),
            _resident(wc.shape, const),
            _resident(wo.shape, const),
        ],
        out_specs=pl.BlockSpec((tm, d), row),
        compiler_params=_cparams("parallel"),
        name="merge",
    )(x2, g, wg, oa, ob, oc, wa, wb, wc, wo)


def _ffn_kernel(x_ref, g_ref, wg_ref, wu_ref, wd_ref, gf_ref, o_ref, *, n_chunks, final_norm):
    x = x_ref[...]
    h = _rms(x, g_ref[...]).astype(BF16)
    fc = wg_ref.shape[1] // n_chunks
    y = x
    for c in range(n_chunks):
        cols = slice(c * fc, (c + 1) * fc)
        gt = jnp.dot(h, wg_ref[:, cols], preferred_element_type=F32)
        up = jnp.dot(h, wu_ref[:, cols], preferred_element_type=F32)
        act = (gt * jax.nn.sigmoid(gt) * up).astype(BF16)
        y = y + jnp.dot(act, wd_ref[cols, :], preferred_element_type=F32)
    if final_norm:
        y = _rms(y, gf_ref[...])
    o_ref[...] = y


def _ffn(x2, g, wg, wu, wd, gf, *, tm, n_chunks, final_norm):
    n, d = x2.shape
    row = lambda i: (i, 0)
    const = lambda i: (0, 0)
    return pl.pallas_call(
        functools.partial(_ffn_kernel, n_chunks=n_chunks, final_norm=final_norm),
        out_shape=jax.ShapeDtypeStruct((n, d), F32),
        grid=(n // tm,),
        in_specs=[
            pl.BlockSpec((tm, d), row),
            pl.BlockSpec((1, d), const),
            _resident(wg.shape, const),
            _resident(wu.shape, const),
            _resident(wd.shape, const),
            pl.BlockSpec((1, d), const),
        ],
        out_specs=pl.BlockSpec((tm, d), row),
        compiler_params=_cparams("parallel"),
        name="ffn",
    )(x2, g, wg, wu, wd, gf)


def _pad_heads(w, heads, width, slot):
    d = w.shape[0]
    w = w.reshape(d, heads, width)
    return jnp.pad(w, ((0, 0), (0, 0), (0, slot - width))).reshape(d, heads * slot)


def _pad_head_rows(w, heads, width, slot):
    d = w.shape[1]
    w = w.reshape(heads, width, d)
    return jnp.pad(w, ((0, 0), (0, slot - width), (0, 0))).reshape(heads * slot, d)


def _ones_col(heads, slot, at):
    col = np.zeros((heads, slot), np.float32)
    col[:, at] = 1.0
    return jnp.asarray(col.reshape(heads * slot, 1))


def _t5_bucket(rel):
    nb = T5_BUCKETS // 2
    max_exact = nb // 2
    n = jnp.abs(rel)
    nf = jnp.maximum(n, 1).astype(jnp.float32)
    large = max_exact + (jnp.log(nf / max_exact) / math.log(T5_MAX_DIST / max_exact)
                         * (nb - max_exact)).astype(jnp.int32)
    large = jnp.minimum(large, nb - 1)
    return jnp.where(rel > 0, nb, 0) + jnp.where(n < max_exact, n, large)


def _tile_rel(n_tiles):
    return (T - 1 - jnp.arange(2 * T))[None, :] - jnp.arange(n_tiles)[:, None] * T


def _rel_vectors(table, idx):
    return jnp.moveaxis(table[idx], -1, 1).astype(F32)[:, :, None, :]


def kernel(x, norm_mix_g, w_in, b_forget, diff_lambda, diff_subln_g, t5_table, b_rel_table,
           w_br_a, w_br_b, w_br_c, w_out, norm_ffn_g, w_gate_up, w_down, final_norm_g):
    b, s, d = x.shape
    depth = w_in.shape[0]
    assert s % T == 0 and T // CHUNK == B_LEFT_CHUNKS
    n = b * s
    scale = HEAD_DIM ** -0.5
    a_qk_w = A_HEADS * 2 * HEAD_DIM
    a_v_w = A_HEADS * A_VDIM
    bw = B_HEADS * HEAD_DIM
    cw = C_HEADS * HEAD_DIM
    d_ff = w_down.shape[1]
    x2 = x.reshape(n, d)

    t5_tiles = _bias_tiles(_rel_vectors(t5_table, _t5_bucket(_tile_rel(A_NEAR))), band=False)
    t5_far = t5_table[T5_BUCKETS // 2 - 1].astype(F32).reshape(A_HEADS, 1, 1)
    band_idx = jnp.clip(_tile_rel(2), -B_REL_CLIP, CHUNK - 1) + B_REL_CLIP
    qk_slots = 2 * (2 * A_HEADS + B_HEADS + C_HEADS)
    brow = jnp.zeros((1, qk_slots * LANE), F32)
    onea = _ones_col(A_HEADS, VA_ROWS, A_VDIM)
    onebc = _ones_col(B_HEADS, VBC_ROWS, HEAD_DIM)

    for l in range(depth):
        w = w_in[l]
        pieces, at = [], 0
        for width in (a_qk_w, a_qk_w, a_v_w, bw, bw, bw, cw, cw, cw, C_HEADS, N_BRANCH * d):
            pieces.append(w[:, at:at + width])
            at += width
        a_q, a_k, a_v, b_q, b_k, b_v, c_q, c_k, c_v, c_f, w_gates = pieces
        w_qk = jnp.concatenate([
            _pad_heads(a_q * scale, 2 * A_HEADS, HEAD_DIM, LANE),
            _pad_heads(a_k, 2 * A_HEADS, HEAD_DIM, LANE),
            _pad_heads(b_q * scale, B_HEADS, HEAD_DIM, LANE),
            _pad_heads(b_k, B_HEADS, HEAD_DIM, LANE),
            _pad_heads(c_q * scale, C_HEADS, HEAD_DIM, LANE),
            _pad_heads(c_k, C_HEADS, HEAD_DIM, LANE),
        ], axis=1).astype(BF16)
        wf = jnp.pad(c_f, ((0, 0), (0, LANE - C_HEADS))).astype(BF16)
        wva = _pad_heads(a_v, A_HEADS, A_VDIM, VA_ROWS).T.astype(BF16)
        wvb = _pad_heads(b_v, B_HEADS, HEAD_DIM, VBC_ROWS).T.astype(BF16)
        wvc = _pad_heads(c_v, C_HEADS, HEAD_DIM, VBC_ROWS).T.astype(BF16)

        qk, cf, vta, vtb, vtc = _in_proj(
            x2, norm_mix_g[l][None], w_qk, brow, wf, wva, wvb, wvc, onea, onebc,
            b=b, tm=min(1024, s), tn=w_qk.shape[1] // 4)

        bf_row = jnp.pad(b_forget[l].astype(F32), (0, LANE - C_HEADS))[None]
        c_slot0 = 2 * (2 * A_HEADS + B_HEADS)
        qc, kc = _augment(cf, bf_row, qk, b=b, q_block=c_slot0 // C_HEADS,
                          k_block=c_slot0 // C_HEADS + 1, tm=T)

        lam_init = 0.8 - 0.6 * math.exp(-0.3 * l)
        o_a = _attn_a(qk, vta, t5_tiles, t5_far, diff_lambda[l].astype(F32),
                      diff_subln_g[l][:, None].astype(F32), b=b, s=s, lam_init=lam_init)

        band_tiles = _bias_tiles(_rel_vectors(b_rel_table[l], band_idx), band=True)
        o_b = _attn_bc(
            _attn_b_kernel, qk, qk, vtb, (band_tiles,),
            (_resident((2, 1, T, T), lambda bi, h, i: (0, h, 0, 0)),),
            b=b, s=s, heads=B_HEADS, q_slot0=4 * A_HEADS, k_slot0=4 * A_HEADS + B_HEADS,
            name="attn_b")

        o_c = _attn_bc(_attn_c_kernel, qc, kc, vtc, (), (), b=b, s=s, heads=C_HEADS,
                       q_slot0=0, k_slot0=0, name="attn_c")

        x2 = _merge(
            x2, norm_mix_g[l][None], w_gates.astype(BF16), o_a, o_b, o_c,
            w_br_a[l].astype(BF16),
            _pad_head_rows(w_br_b[l], B_HEADS, HEAD_DIM, LANE).astype(BF16),
            _pad_head_rows(w_br_c[l], C_HEADS, HEAD_DIM, LANE).astype(BF16),
            w_out[l].astype(BF16), tm=min(512, n))

        x2 = _ffn(
            x2, norm_ffn_g[l][None], w_gate_up[l][:, :d_ff].astype(BF16),
            w_gate_up[l][:, d_ff:].astype(BF16), w_down[l].astype(BF16),
            final_norm_g[None], tm=min(512, n), n_chunks=2, final_norm=(l == depth - 1))

    return x2.reshape(b, s, d)
```

```python
import functools
import math

import numpy as np
import jax
import jax.numpy as jnp
from jax import lax
from jax.experimental import pallas as pl
from jax.experimental.pallas import tpu as pltpu

CHUNK = 64
HEAD_DIM = 64
A_HEADS = 4
A_VDIM = 2 * HEAD_DIM
B_HEADS = 8
B_LEFT_CHUNKS = 8
B_REL_CLIP = 128
C_HEADS = 8
T5_BUCKETS = 32
T5_MAX_DIST = 2048
N_BRANCH = 3
RMS_EPS = 1e-6
NEG_INF = -1e30
LOG2E = math.log2(math.e)

LANE = 128
BF16_ROWS = 16
T = 512
A_NEAR = 3
VA_ROWS = A_VDIM + BF16_ROWS
VBC_ROWS = HEAD_DIM + BF16_ROWS
C_PARTS = 3
VMEM_LIMIT = 56 * 1024 * 1024

F32 = jnp.float32
BF16 = jnp.bfloat16


def _cparams(*sem):
    return pltpu.CompilerParams(dimension_semantics=sem, vmem_limit_bytes=VMEM_LIMIT)


def _resident(block_shape, index_map):
    return pl.BlockSpec(block_shape, index_map, pipeline_mode=pl.Buffered(1))


def _rms(x, g, axis=-1):
    y = x * lax.rsqrt(jnp.mean(x * x, axis=axis, keepdims=True) + RMS_EPS)
    return y * g


def _dot_nt(a, b):
    return lax.dot_general(a, b, (((1,), (1,)), ((), ())), preferred_element_type=F32)


def _in_proj_kernel(x_ref, g_ref, w_ref, brow_ref, wf_ref, wva_ref, wvb_ref, wvc_ref,
                    onea_ref, onebc_ref, o_ref, cf_ref, va_ref, vb_ref, vc_ref, h_ref):
    @pl.when(pl.program_id(1) == 0)
    def _():
        h = _rms(x_ref[...], g_ref[...]).astype(BF16)
        h_ref[...] = h
        cf_ref[...] = jnp.dot(h, wf_ref[...], preferred_element_type=F32)
        va_ref[...] = (_dot_nt(wva_ref[...], h) + onea_ref[...]).astype(BF16)
        vb_ref[...] = (_dot_nt(wvb_ref[...], h) + onebc_ref[...]).astype(BF16)
        vc_ref[...] = (_dot_nt(wvc_ref[...], h) + onebc_ref[...]).astype(BF16)

    acc = jnp.dot(h_ref[...], w_ref[...], preferred_element_type=F32)
    o_ref[...] = (acc + brow_ref[...]).astype(BF16)


def _in_proj(x2, g, w_qk, brow, wf, wva, wvb, wvc, onea, onebc, *, b, tm, tn):
    n, d = x2.shape
    s = n // b
    nt = s // tm
    ncol = w_qk.shape[1]
    ra, rbc = wva.shape[0], wvb.shape[0]
    const = lambda i, j: (0, 0)
    vt_map = lambda i, j: (i // nt, i % nt)
    return pl.pallas_call(
        _in_proj_kernel,
        out_shape=(jax.ShapeDtypeStruct((n, ncol), BF16), jax.ShapeDtypeStruct((n, LANE), F32),
                   jax.ShapeDtypeStruct((b * ra, s), BF16), jax.ShapeDtypeStruct((b * rbc, s), BF16),
                   jax.ShapeDtypeStruct((b * rbc, s), BF16)),
        grid=(n // tm, ncol // tn),
        in_specs=[
            pl.BlockSpec((tm, d), lambda i, j: (i, 0)),
            pl.BlockSpec((1, d), const),
            pl.BlockSpec((d, tn), lambda i, j: (0, j)),
            pl.BlockSpec((1, tn), lambda i, j: (0, j)),
            pl.BlockSpec((d, LANE), const),
            _resident((ra, d), const),
            _resident((rbc, d), const),
            _resident((rbc, d), const),
            pl.BlockSpec((ra, 1), const),
            pl.BlockSpec((rbc, 1), const),
        ],
        out_specs=(
            pl.BlockSpec((tm, tn), lambda i, j: (i, j)),
            pl.BlockSpec((tm, LANE), lambda i, j: (i, 0)),
            pl.BlockSpec((ra, tm), vt_map),
            pl.BlockSpec((rbc, tm), vt_map),
            pl.BlockSpec((rbc, tm), vt_map),
        ),
        scratch_shapes=[pltpu.VMEM((tm, d), BF16)],
        compiler_params=_cparams("parallel", "arbitrary"),
        name="in_proj",
    )(x2, g, w_qk, brow, wf, wva, wvb, wvc, onea, onebc)


def _augment_kernel(cf_ref, bf_ref, tri_ref, q_ref, k_ref, pq_ref, pk_ref, qrow_ref, krow_ref,
                    qo_ref, ko_ref, carry_ref):
    @pl.when(pl.program_id(1) == 0)
    def _():
        carry_ref[...] = jnp.zeros_like(carry_ref)

    x = cf_ref[...] + bf_ref[...]
    log_f = -(jnp.maximum(-x, 0.0) + jnp.log1p(jnp.exp(-jnp.abs(x))))
    c = jnp.dot(tri_ref[...], log_f, precision=lax.Precision.HIGHEST,
                preferred_element_type=F32) + carry_ref[...]
    carry_ref[...] = c[c.shape[0] - 1:, :]
    parts, rest = [], c * LOG2E
    for _ in range(C_PARTS):
        piece = rest.astype(BF16)
        parts.append(piece)
        rest = rest - piece.astype(F32)
    parts = jnp.concatenate(parts, axis=1)
    placed_q = jnp.dot(parts, pq_ref[...], preferred_element_type=F32)
    placed_k = jnp.dot(parts, pk_ref[...], preferred_element_type=F32)
    qo_ref[...] = (q_ref[...].astype(F32) + placed_q + qrow_ref[...]).astype(BF16)
    ko_ref[...] = (k_ref[...].astype(F32) - placed_k + krow_ref[...]).astype(BF16)


def _augment_constants():
    w = C_HEADS * LANE
    pq = np.zeros((C_PARTS * LANE, w), np.float32)
    pk = np.zeros((C_PARTS * LANE, w), np.float32)
    qrow = np.zeros((1, w), np.float32)
    krow = np.zeros((1, w), np.float32)
    for h in range(C_HEADS):
        for p in range(C_PARTS):
            pq[p * LANE + h, h * LANE + HEAD_DIM + p] = 1.0
            pk[p * LANE + h, h * LANE + HEAD_DIM + C_PARTS + p] = 1.0
            krow[0, h * LANE + HEAD_DIM + p] = 1.0
            qrow[0, h * LANE + HEAD_DIM + C_PARTS + p] = 1.0
    return jnp.asarray(pq, BF16), jnp.asarray(pk, BF16), jnp.asarray(qrow), jnp.asarray(krow)


def _augment(cf, bf_row, proj, *, b, q_block, k_block, tm):
    n = cf.shape[0]
    nt = n // b // tm
    w = C_HEADS * LANE
    tri = jnp.asarray(np.tril(np.ones((tm, tm), np.float32)))
    pq, pk, qrow, krow = _augment_constants()
    const = lambda bi, i: (0, 0)
    row = lambda bi, i: (bi * nt + i, 0)
    return pl.pallas_call(
        _augment_kernel,
        out_shape=(jax.ShapeDtypeStruct((n, w), BF16), jax.ShapeDtypeStruct((n, w), BF16)),
        grid=(b, nt),
        in_specs=[
            pl.BlockSpec((tm, LANE), row),
            pl.BlockSpec((1, LANE), const),
            pl.BlockSpec((tm, tm), const),
            pl.BlockSpec((tm, w), lambda bi, i: (bi * nt + i, q_block)),
            pl.BlockSpec((tm, w), lambda bi, i: (bi * nt + i, k_block)),
            pl.BlockSpec(pq.shape, const),
            pl.BlockSpec(pk.shape, const),
            pl.BlockSpec((1, w), const),
            pl.BlockSpec((1, w), const),
        ],
        out_specs=(pl.BlockSpec((tm, w), row), pl.BlockSpec((tm, w), row)),
        scratch_shapes=[pltpu.VMEM((1, LANE), F32)],
        compiler_params=_cparams("arbitrary", "arbitrary"),
        name="augment",
    )(cf, bf_row, tri, proj, proj, pq, pk, qrow, krow)


def _bias_tile_kernel(x_ref, o_ref, *, distance, band):
    d = distance(*(pl.program_id(a) for a in range(len(o_ref.shape) - 2)))
    x = jnp.broadcast_to(x_ref[0, 0], (T, 2 * T))
    tile = pltpu.roll(x, T + 1, 1, stride=1, stride_axis=0)[:, :T] * LOG2E
    ahead = (lax.broadcasted_iota(jnp.int32, (T, T), 0) // CHUNK
             - lax.broadcasted_iota(jnp.int32, (T, T), 1) // CHUNK
             - d * (T // CHUNK))
    if band:
        tile = jnp.where(ahead >= -B_LEFT_CHUNKS, tile, NEG_INF)
    o_ref[...] = jnp.where(ahead <= 0, tile, NEG_INF).reshape(o_ref.shape)


def _bias_tiles(x, lead, distance, d_min, *, band, name):
    heads = x.shape[1]
    zeros = (0,) * 2
    return pl.pallas_call(
        functools.partial(_bias_tile_kernel, distance=distance, band=band),
        out_shape=jax.ShapeDtypeStruct((heads, *lead, T, T), F32),
        grid=(heads, *lead),
        in_specs=[pl.BlockSpec((1, 1, 1, 2 * T),
                               lambda h, *idx: (distance(h, *idx) - d_min, h, 0, 0))],
        out_specs=pl.BlockSpec((1,) * (1 + len(lead)) + (T, T), lambda h, *idx: (h, *idx, *zeros)),
        compiler_params=_cparams(*(("parallel",) * (1 + len(lead)))),
        name=name,
    )(x)


def _t5_distance(h, parity, which, half):
    return parity + 2 * (1 - which) - half


def _band_distance(h, half):
    return 1 - half


def _softmax_step(s, vt, m_ref, acc_ref):
    rows = vt.shape[0]
    m_prev = m_ref[...]
    m_new = jnp.maximum(m_prev, jnp.max(s, axis=0, keepdims=True))
    alpha = jnp.exp2(m_prev - m_new)
    p = jnp.exp2(s - m_new).astype(BF16)
    acc_ref[:rows, :] = acc_ref[:rows, :] * alpha + jnp.dot(vt, p, preferred_element_type=F32)
    m_ref[...] = m_new


def _keys(j, n_blocks=1):
    return pl.ds(pl.multiple_of(j * T, T), n_blocks * T)


def _step_keys(t):
    return _keys(2 * t, 2)


def _init(m_ref, acc_ref):
    m_ref[...] = jnp.full_like(m_ref, NEG_INF)
    acc_ref[...] = jnp.zeros_like(acc_ref)


def _attn_a_kernel(q_ref, k_ref, vt_ref, tile_ref, lp_ref, sg_ref, o_ref,
                   m0_ref, m1_ref, acc0_ref, acc1_ref, s0a_ref, s0b_ref, s1a_ref, s1b_ref,
                   *, lam_init):
    i = pl.program_id(2)
    streams = ((m0_ref, acc0_ref), (m1_ref, acc1_ref))
    buffers = ((s0a_ref, s0b_ref), (s1a_ref, s1b_ref))
    for m_ref, acc_ref in streams:
        _init(m_ref, acc_ref)
    q = q_ref[...]
    parity = i % 2
    n_steps = i // 2 + 1
    n_far = jnp.maximum(n_steps - 2, 0)

    def fill(which, t):
        for sub in range(2):
            lanes = slice(sub * LANE, (sub + 1) * LANE)
            buffers[sub][which][...] = _dot_nt(k_ref[_step_keys(t), lanes], q[:, lanes])

    def consume(which, t, near=None):
        vt = vt_ref[:, _step_keys(t)]
        for sub, (m_ref, acc_ref) in enumerate(streams):
            s = buffers[sub][which][...]
            if near is not None:
                s = s + tile_ref[0, parity, near]
            _softmax_step(s, vt, m_ref, acc_ref)

    fill(0, 0)

    def far_body(u, carry):
        fill(1, 2 * u + 1)
        consume(0, 2 * u)
        fill(0, 2 * u + 2)
        consume(1, 2 * u + 1)
        return carry

    lax.fori_loop(0, n_far // 2, far_body, 0)

    @pl.when(n_steps == 1)
    def _():
        consume(0, 0, near=1)

    @pl.when(jnp.logical_and(n_steps >= 2, n_far % 2 == 0))
    def _():
        fill(1, n_steps - 1)
        consume(0, n_steps - 2, near=0)
        consume(1, n_steps - 1, near=1)

    @pl.when(n_far % 2 == 1)
    def _():
        fill(1, n_far)
        consume(0, n_far - 1)
        fill(0, n_far + 1)
        consume(1, n_far, near=0)
        consume(0, n_far + 1, near=1)

    lp = lp_ref[...]
    lam = (jnp.exp(jnp.sum(lp[0:1] * lp[1:2], axis=-1, keepdims=True))
           - jnp.exp(jnp.sum(lp[2:3] * lp[3:4], axis=-1, keepdims=True)) + lam_init)
    acc0 = acc0_ref[...]
    acc1 = acc1_ref[...]
    ot = (acc0[:A_VDIM] / acc0[A_VDIM:A_VDIM + 1]
          - lam * (acc1[:A_VDIM] / acc1[A_VDIM:A_VDIM + 1]))
    ot = _rms(ot, sg_ref[...], axis=0) * (1.0 - lam_init)
    o_ref[...] = ot.T.astype(BF16)


def _attn_a(qk, vt, tiles, lp, sg, *, b, s, lam_init):
    nq = s // T
    wa = 2 * LANE
    return pl.pallas_call(
        functools.partial(_attn_a_kernel, lam_init=lam_init),
        out_shape=jax.ShapeDtypeStruct((b * s, A_HEADS * A_VDIM), BF16),
        grid=(b, A_HEADS, nq),
        in_specs=[
            pl.BlockSpec((T, wa), lambda bi, h, i: (bi * nq + i, h)),
            _resident((s, wa), lambda bi, h, i: (bi, A_HEADS + h)),
            _resident((VA_ROWS, s), lambda bi, h, i: (bi * A_HEADS + h, 0)),
            _resident((1, 2, 2, 2 * T, T), lambda bi, h, i: (h, 0, 0, 0, 0)),
            pl.BlockSpec((4, HEAD_DIM), lambda bi, h, i: (0, 0)),
            pl.BlockSpec((A_VDIM, 1), lambda bi, h, i: (0, 0)),
        ],
        out_specs=pl.BlockSpec((T, A_VDIM), lambda bi, h, i: (bi * nq + i, h)),
        scratch_shapes=[pltpu.VMEM((1, T), F32), pltpu.VMEM((1, T), F32),
                        pltpu.VMEM((VA_ROWS, T), F32), pltpu.VMEM((VA_ROWS, T), F32)]
        + [pltpu.VMEM((2 * T, T), F32)] * 4,
        compiler_params=_cparams("parallel", "parallel", "arbitrary"),
        name="attn_a",
    )(qk, qk, vt, tiles, lp, sg)


def _normalized(acc_ref):
    acc = acc_ref[...]
    return (acc / acc[HEAD_DIM:HEAD_DIM + 1]).T.astype(BF16)


def _attn_b_kernel(q_ref, k_ref, vt_ref, tile_ref, o_ref, m_ref, acc_ref):
    i = pl.program_id(2)
    _init(m_ref, acc_ref)
    q = q_ref[...]

    def step(keys, bias):
        _softmax_step(_dot_nt(k_ref[keys, :], q) + bias, vt_ref[:, keys], m_ref, acc_ref)

    @pl.when(i >= 1)
    def _():
        step(_keys(i - 1, 2), tile_ref[0])

    @pl.when(i == 0)
    def _():
        step(_keys(0), tile_ref[0, T:, :])

    o_ref[...] = _normalized(acc_ref)


def _attn_c_kernel(q_ref, k_ref, vt_ref, o_ref, m_ref, acc_ref, sa_ref, sb_ref):
    i = pl.program_id(2)
    _init(m_ref, acc_ref)
    q = q_ref[...]
    n_plain = i // 2

    def fill(s_ref, t):
        s_ref[...] = _dot_nt(k_ref[_step_keys(t), :], q)

    def consume(s_ref, t, masked=False):
        s = s_ref[...]
        if masked:
            ahead = (lax.broadcasted_iota(jnp.int32, s.shape, 0)
                     - lax.broadcasted_iota(jnp.int32, s.shape, 1))
            s = jnp.where(ahead <= (i - 2 * t) * T, s, NEG_INF)
        _softmax_step(s, vt_ref[:, _step_keys(t)], m_ref, acc_ref)

    fill(sa_ref, 0)

    def body(u, carry):
        fill(sb_ref, 2 * u + 1)
        consume(sa_ref, 2 * u)
        fill(sa_ref, 2 * u + 2)
        consume(sb_ref, 2 * u + 1)
        return carry

    lax.fori_loop(0, n_plain // 2, body, 0)

    @pl.when(n_plain % 2 == 0)
    def _():
        consume(sa_ref, n_plain, masked=True)

    @pl.when(n_plain % 2 == 1)
    def _():
        fill(sb_ref, n_plain)
        consume(sa_ref, n_plain - 1)
        consume(sb_ref, n_plain, masked=True)

    o_ref[...] = _normalized(acc_ref)


def _attn_bc(kernel, q_arr, k_arr, vt, extra, extra_specs, *, b, s, heads, q_slot0, k_slot0, name,
             score_buffers=0):
    nq = s // T
    return pl.pallas_call(
        kernel,
        out_shape=jax.ShapeDtypeStruct((b * s, heads * LANE), BF16),
        grid=(b, heads, nq),
        in_specs=[
            pl.BlockSpec((T, LANE), lambda bi, h, i: (bi * nq + i, q_slot0 + h)),
            _resident((s, LANE), lambda bi, h, i: (bi, k_slot0 + h)),
            _resident((VBC_ROWS, s), lambda bi, h, i: (bi * heads + h, 0)),
            *extra_specs,
        ],
        out_specs=pl.BlockSpec((T, LANE), lambda bi, h, i: (bi * nq + i, h)),
        scratch_shapes=[pltpu.VMEM((1, T), F32), pltpu.VMEM((LANE, T), F32)]
        + [pltpu.VMEM((2 * T, T), F32)] * score_buffers,
        compiler_params=_cparams("parallel", "parallel", "arbitrary"),
        name=name,
    )(q_arr, k_arr, vt, *extra)


def _merge_kernel(x_ref, g_ref, wg_ref, oa_ref, ob_ref, oc_ref, wa_ref, wb_ref, wc_ref,
                  wo_ref, o_ref):
    x = x_ref[...]
    d = x.shape[1]
    h = _rms(x, g_ref[...]).astype(BF16)
    gates = jax.nn.sigmoid(jnp.dot(h, wg_ref[...], preferred_element_type=F32))
    merged = None
    for n, (o_br, w_br) in enumerate(((oa_ref, wa_ref), (ob_ref, wb_ref), (oc_ref, wc_ref))):
        term = gates[:, n * d:(n + 1) * d] * jnp.dot(o_br[...], w_br[...],
                                                     preferred_element_type=F32)
        merged = term if merged is None else merged + term
    o_ref[...] = x + jnp.dot(merged.astype(BF16), wo_ref[...], preferred_element_type=F32)


def _merge(x2, g, wg, oa, ob, oc, wa, wb, wc, wo, *, tm):
    n, d = x2.shape
    row = lambda i: (i, 0)
    const = lambda i: (0, 0)
    return pl.pallas_call(
        _merge_kernel,
        out_shape=jax.ShapeDtypeStruct((n, d), F32),
        grid=(n // tm,),
        in_specs=[
            pl.BlockSpec((tm, d), row),
            pl.BlockSpec((1, d), const),
            _resident(wg.shape, const),
            pl.BlockSpec((tm, oa.shape[1]), row),
            pl.BlockSpec((tm, ob.shape[1]), row),
            pl.BlockSpec((tm, oc.shape[1]), row),
            _resident(wa.shape, const),
            _resident(wb.shape, const),
            _resident(wc.shape, const),
            _resident(wo.shape, const),
        ],
        out_specs=pl.BlockSpec((tm, d), row),
        compiler_params=_cparams("parallel"),
        name="merge",
    )(x2, g, wg, oa, ob, oc, wa, wb, wc, wo)


def _ffn_kernel(x_ref, g_ref, wg_ref, wu_ref, wd_ref, gf_ref, o_ref, *, n_chunks, final_norm):
    x = x_ref[...]
    h = _rms(x, g_ref[...]).astype(BF16)
    fc = wg_ref.shape[1] // n_chunks
    y = x
    for c in range(n_chunks):
        cols = slice(c * fc, (c + 1) * fc)
        gt = jnp.dot(h, wg_ref[:, cols], preferred_element_type=F32)
        up = jnp.dot(h, wu_ref[:, cols], preferred_element_type=F32)
        act = (gt * jax.nn.sigmoid(gt) * up).astype(BF16)
        y = y + jnp.dot(act, wd_ref[cols, :], preferred_element_type=F32)
    if final_norm:
        y = _rms(y, gf_ref[...])
    o_ref[...] = y


def _ffn(x2, g, wg, wu, wd, gf, *, tm, n_chunks, final_norm):
    n, d = x2.shape
    row = lambda i: (i, 0)
    const = lambda i: (0, 0)
    return pl.pallas_call(
        functools.partial(_ffn_kernel, n_chunks=n_chunks, final_norm=final_norm),
        out_shape=jax.ShapeDtypeStruct((n, d), F32),
        grid=(n // tm,),
        in_specs=[
            pl.BlockSpec((tm, d), row),
            pl.BlockSpec((1, d), const),
            _resident(wg.shape, const),
            _resident(wu.shape, const),
            _resident(wd.shape, const),
            pl.BlockSpec((1, d), const),
        ],
        out_specs=pl.BlockSpec((tm, d), row),
        compiler_params=_cparams("parallel"),
        name="ffn",
    )(x2, g, wg, wu, wd, gf)


def _pad_heads(w, heads, width, slot):
    d = w.shape[0]
    w = w.reshape(d, heads, width)
    return jnp.pad(w, ((0, 0), (0, 0), (0, slot - width))).reshape(d, heads * slot)


def _pad_head_rows(w, heads, width, slot):
    d = w.shape[1]
    w = w.reshape(heads, width, d)
    return jnp.pad(w, ((0, 0), (0, slot - width), (0, 0))).reshape(heads * slot, d)


def _ones_col(heads, slot, at):
    col = np.zeros((heads, slot), np.float32)
    col[:, at] = 1.0
    return jnp.asarray(col.reshape(heads * slot, 1))


def _t5_bucket(rel):
    nb = T5_BUCKETS // 2
    max_exact = nb // 2
    n = jnp.abs(rel)
    nf = jnp.maximum(n, 1).astype(jnp.float32)
    large = max_exact + (jnp.log(nf / max_exact) / math.log(T5_MAX_DIST / max_exact)
                         * (nb - max_exact)).astype(jnp.int32)
    large = jnp.minimum(large, nb - 1)
    return jnp.where(rel > 0, nb, 0) + jnp.where(n < max_exact, n, large)


def _tile_rel(d_min, d_max):
    return ((T - 1 - jnp.arange(2 * T))[None, :]
            - jnp.arange(d_min, d_max + 1)[:, None] * T)


def _rel_vectors(table, idx):
    return jnp.moveaxis(table[idx], -1, 1).astype(F32)[:, :, None, :]


def kernel(x, norm_mix_g, w_in, b_forget, diff_lambda, diff_subln_g, t5_table, b_rel_table,
           w_br_a, w_br_b, w_br_c, w_out, norm_ffn_g, w_gate_up, w_down, final_norm_g):
    b, s, d = x.shape
    depth = w_in.shape[0]
    assert s % (2 * T) == 0 and T // CHUNK == B_LEFT_CHUNKS
    n = b * s
    scale = HEAD_DIM ** -0.5 * LOG2E
    a_qk_w = A_HEADS * 2 * HEAD_DIM
    a_v_w = A_HEADS * A_VDIM
    bw = B_HEADS * HEAD_DIM
    cw = C_HEADS * HEAD_DIM
    d_ff = w_down.shape[1]
    x2 = x.reshape(n, d)

    t5_far = t5_table[T5_BUCKETS // 2 - 1].astype(F32)
    t5_tiles = _bias_tiles(
        _rel_vectors(t5_table, _t5_bucket(_tile_rel(-1, A_NEAR))) - t5_far[None, :, None, None],
        (2, 2, 2), _t5_distance, -1, band=False, name="bias_tiles_t5",
    ).reshape(A_HEADS, 2, 2, 2 * T, T)
    far_pieces, rest = [], t5_far * LOG2E
    for _ in range(C_PARTS):
        piece = rest.astype(BF16).astype(F32)
        far_pieces.append(piece)
        rest = rest - piece
    far_pieces = jnp.repeat(jnp.stack(far_pieces, axis=-1), 2, axis=0)
    spare = slice(HEAD_DIM, HEAD_DIM + C_PARTS)
    brow_qa = jnp.zeros((2 * A_HEADS, LANE), F32).at[:, spare].set(1.0)
    brow_ka = jnp.zeros((2 * A_HEADS, LANE), F32).at[:, spare].set(far_pieces)
    bc_slots = 2 * (B_HEADS + C_HEADS)
    brow = jnp.concatenate([brow_qa.reshape(1, -1), brow_ka.reshape(1, -1),
                            jnp.zeros((1, bc_slots * LANE), F32)], axis=1)
    band_idx = jnp.clip(_tile_rel(0, 1), -B_REL_CLIP, CHUNK - 1) + B_REL_CLIP
    onea = _ones_col(A_HEADS, VA_ROWS, A_VDIM)
    onebc = _ones_col(B_HEADS, VBC_ROWS, HEAD_DIM)

    for l in range(depth):
        w = w_in[l]
        pieces, at = [], 0
        for width in (a_qk_w, a_qk_w, a_v_w, bw, bw, bw, cw, cw, cw, C_HEADS, N_BRANCH * d):
            pieces.append(w[:, at:at + width])
            at += width
        a_q, a_k, a_v, b_q, b_k, b_v, c_q, c_k, c_v, c_f, w_gates = pieces
        w_qk = jnp.concatenate([
            _pad_heads(a_q * scale, 2 * A_HEADS, HEAD_DIM, LANE),
            _pad_heads(a_k, 2 * A_HEADS, HEAD_DIM, LANE),
            _pad_heads(b_q * scale, B_HEADS, HEAD_DIM, LANE),
            _pad_heads(b_k, B_HEADS, HEAD_DIM, LANE),
            _pad_heads(c_q * scale, C_HEADS, HEAD_DIM, LANE),
            _pad_heads(c_k, C_HEADS, HEAD_DIM, LANE),
        ], axis=1).astype(BF16)
        wf = jnp.pad(c_f, ((0, 0), (0, LANE - C_HEADS))).astype(BF16)
        wva = _pad_heads(a_v, A_HEADS, A_VDIM, VA_ROWS).T.astype(BF16)
        wvb = _pad_heads(b_v, B_HEADS, HEAD_DIM, VBC_ROWS).T.astype(BF16)
        wvc = _pad_heads(c_v, C_HEADS, HEAD_DIM, VBC_ROWS).T.astype(BF16)

        qk, cf, vta, vtb, vtc = _in_proj(
            x2, norm_mix_g[l][None], w_qk, brow, wf, wva, wvb, wvc, onea, onebc,
            b=b, tm=min(1024, s), tn=w_qk.shape[1] // 4)

        bf_row = jnp.pad(b_forget[l].astype(F32), (0, LANE - C_HEADS))[None]
        c_slot0 = 2 * (2 * A_HEADS + B_HEADS)
        qc, kc = _augment(cf, bf_row, qk, b=b, q_block=c_slot0 // C_HEADS,
                          k_block=c_slot0 // C_HEADS + 1, tm=T)

        lam_init = 0.8 - 0.6 * math.exp(-0.3 * l)
        o_a = _attn_a(qk, vta, t5_tiles, diff_lambda[l].astype(F32),
                      diff_subln_g[l][:, None].astype(F32), b=b, s=s, lam_init=lam_init)

        band_tiles = _bias_tiles(
            _rel_vectors(b_rel_table[l], band_idx), (2,), _band_distance, 0, band=True,
            name="bias_tiles_band").reshape(B_HEADS, 2 * T, T)
        o_b = _attn_bc(
            _attn_b_kernel, qk, qk, vtb, (band_tiles,),
            (_resident((1, 2 * T, T), lambda bi, h, i: (h, 0, 0)),),
            b=b, s=s, heads=B_HEADS, q_slot0=4 * A_HEADS, k_slot0=4 * A_HEADS + B_HEADS,
            name="attn_b")

        o_c = _attn_bc(_attn_c_kernel, qc, kc, vtc, (), (), b=b, s=s, heads=C_HEADS,
                       q_slot0=0, k_slot0=0, name="attn_c", score_buffers=2)

        x2 = _merge(
            x2, norm_mix_g[l][None], w_gates.astype(BF16), o_a, o_b, o_c,
            w_br_a[l].astype(BF16),
            _pad_head_rows(w_br_b[l], B_HEADS, HEAD_DIM, LANE).astype(BF16),
            _pad_head_rows(w_br_c[l], C_HEADS, HEAD_DIM, LANE).astype(BF16),
            w_out[l].astype(BF16), tm=min(512, n))

        x2 = _ffn(
            x2, norm_ffn_g[l][None], w_gate_up[l][:, :d_ff].astype(BF16),
            w_gate_up[l][:, d_ff:].astype(BF16), w_down[l].astype(BF16),
            final_norm_g[None], tm=min(512, n), n_chunks=2, final_norm=(l == depth - 1))

    return x2.reshape(b, s, d)
```

```python
import functools
import math

import numpy as np
import jax
import jax.numpy as jnp
from jax import lax
from jax.experimental import pallas as pl
from jax.experimental.pallas import tpu as pltpu

CHUNK = 64
HEAD_DIM = 64
A_HEADS = 4
A_VDIM = 2 * HEAD_DIM
B_HEADS = 8
B_LEFT_CHUNKS = 8
B_REL_CLIP = 128
C_HEADS = 8
T5_BUCKETS = 32
T5_MAX_DIST = 2048
N_BRANCH = 3
RMS_EPS = 1e-6
NEG_INF = -1e30
LOG2E = math.log2(math.e)

LANE = 128
BF16_ROWS = 16
T = 512
A_NEAR = 3
VA_ROWS = A_VDIM + BF16_ROWS
VBC_ROWS = HEAD_DIM + BF16_ROWS
C_PARTS = 3
VMEM_LIMIT = 56 * 1024 * 1024

F32 = jnp.float32
BF16 = jnp.bfloat16


def _cparams(*sem):
    return pltpu.CompilerParams(dimension_semantics=sem, vmem_limit_bytes=VMEM_LIMIT)


def _resident(block_shape, index_map):
    return pl.BlockSpec(block_shape, index_map, pipeline_mode=pl.Buffered(1))


def _rms(x, g, axis=-1):
    y = x * lax.rsqrt(jnp.mean(x * x, axis=axis, keepdims=True) + RMS_EPS)
    return y * g


def _dot_nt(a, b):
    return lax.dot_general(a, b, (((1,), (1,)), ((), ())), preferred_element_type=F32)


def _in_proj_kernel(x_ref, g_ref, w_ref, brow_ref, wf_ref, bf_ref, tri_ref, pq_ref, pk_ref,
                    wva_ref, wvb_ref, wvc_ref, onea_ref, onebc_ref,
                    o_ref, va_ref, vb_ref, vc_ref, h_ref, parts_ref, carry_ref,
                    *, row_tiles_per_seq, cq_tile, ck_tile):
    i, j = pl.program_id(0), pl.program_id(1)

    @pl.when(j == 0)
    def _():
        h = _rms(x_ref[...], g_ref[...]).astype(BF16)
        h_ref[...] = h
        va_ref[...] = (_dot_nt(wva_ref[...], h) + onea_ref[...]).astype(BF16)
        vb_ref[...] = (_dot_nt(wvb_ref[...], h) + onebc_ref[...]).astype(BF16)
        vc_ref[...] = (_dot_nt(wvc_ref[...], h) + onebc_ref[...]).astype(BF16)

        @pl.when(i % row_tiles_per_seq == 0)
        def _():
            carry_ref[...] = jnp.zeros_like(carry_ref)

        x = jnp.dot(h, wf_ref[...], preferred_element_type=F32) + bf_ref[...]
        log_f = -(jnp.maximum(-x, 0.0) + jnp.log1p(jnp.exp(-jnp.abs(x))))
        c = jnp.dot(tri_ref[...], log_f, precision=lax.Precision.HIGHEST,
                    preferred_element_type=F32) + carry_ref[...]
        carry_ref[...] = c[c.shape[0] - 1:, :]
        rest = c * LOG2E
        for p in range(C_PARTS):
            piece = rest.astype(BF16)
            parts_ref[:, p * LANE:(p + 1) * LANE] = piece
            rest = rest - piece.astype(F32)

    acc = jnp.dot(h_ref[...], w_ref[...], preferred_element_type=F32) + brow_ref[...]

    @pl.when(j == cq_tile)
    def _():
        placed = jnp.dot(parts_ref[...], pq_ref[...], preferred_element_type=F32)
        o_ref[...] = (acc + placed).astype(BF16)

    @pl.when(j == ck_tile)
    def _():
        placed = jnp.dot(parts_ref[...], pk_ref[...], preferred_element_type=F32)
        o_ref[...] = (acc - placed).astype(BF16)

    @pl.when(jnp.logical_and(j != cq_tile, j != ck_tile))
    def _():
        o_ref[...] = acc.astype(BF16)


def _placement_constants():
    w = C_HEADS * LANE
    pq = np.zeros((C_PARTS * LANE, w), np.float32)
    pk = np.zeros((C_PARTS * LANE, w), np.float32)
    for h in range(C_HEADS):
        for p in range(C_PARTS):
            pq[p * LANE + h, h * LANE + HEAD_DIM + p] = 1.0
            pk[p * LANE + h, h * LANE + HEAD_DIM + C_PARTS + p] = 1.0
    return jnp.asarray(pq, BF16), jnp.asarray(pk, BF16)


def _in_proj(x2, g, w_qk, brow, wf, bf_row, wva, wvb, wvc, onea, onebc, *, b, tm, cq_tile):
    n, d = x2.shape
    s = n // b
    nt = s // tm
    tn = C_HEADS * LANE
    ncol = w_qk.shape[1]
    ra, rbc = wva.shape[0], wvb.shape[0]
    tri = jnp.asarray(np.tril(np.ones((tm, tm), np.float32)))
    pq, pk = _placement_constants()
    const = lambda i, j: (0, 0)
    vt_map = lambda i, j: (i // nt, i % nt)
    return pl.pallas_call(
        functools.partial(_in_proj_kernel, row_tiles_per_seq=nt, cq_tile=cq_tile,
                          ck_tile=cq_tile + 1),
        out_shape=(jax.ShapeDtypeStruct((n, ncol), BF16),
                   jax.ShapeDtypeStruct((b * ra, s), BF16), jax.ShapeDtypeStruct((b * rbc, s), BF16),
                   jax.ShapeDtypeStruct((b * rbc, s), BF16)),
        grid=(n // tm, ncol // tn),
        in_specs=[
            pl.BlockSpec((tm, d), lambda i, j: (i, 0)),
            pl.BlockSpec((1, d), const),
            pl.BlockSpec((d, tn), lambda i, j: (0, j)),
            pl.BlockSpec((1, tn), lambda i, j: (0, j)),
            pl.BlockSpec((d, LANE), const),
            pl.BlockSpec((1, LANE), const),
            _resident((tm, tm), const),
            _resident(pq.shape, const),
            _resident(pk.shape, const),
            _resident((ra, d), const),
            _resident((rbc, d), const),
            _resident((rbc, d), const),
            pl.BlockSpec((ra, 1), const),
            pl.BlockSpec((rbc, 1), const),
        ],
        out_specs=(
            pl.BlockSpec((tm, tn), lambda i, j: (i, j)),
            pl.BlockSpec((ra, tm), vt_map),
            pl.BlockSpec((rbc, tm), vt_map),
            pl.BlockSpec((rbc, tm), vt_map),
        ),
        scratch_shapes=[pltpu.VMEM((tm, d), BF16), pltpu.VMEM((tm, C_PARTS * LANE), BF16),
                        pltpu.VMEM((1, LANE), F32)],
        compiler_params=_cparams("arbitrary", "arbitrary"),
        name="in_proj",
    )(x2, g, w_qk, brow, wf, bf_row, tri, pq, pk, wva, wvb, wvc, onea, onebc)


def _bias_tile_kernel(x_ref, o_ref, *, distance, band):
    d = distance(*(pl.program_id(a) for a in range(len(o_ref.shape) - 2)))
    x = jnp.broadcast_to(x_ref[0, 0], (T, 2 * T))
    tile = pltpu.roll(x, T + 1, 1, stride=1, stride_axis=0)[:, :T] * LOG2E
    ahead = (lax.broadcasted_iota(jnp.int32, (T, T), 0) // CHUNK
             - lax.broadcasted_iota(jnp.int32, (T, T), 1) // CHUNK
             - d * (T // CHUNK))
    if band:
        tile = jnp.where(ahead >= -B_LEFT_CHUNKS, tile, NEG_INF)
    o_ref[...] = jnp.where(ahead <= 0, tile, NEG_INF).reshape(o_ref.shape)


def _bias_tiles(x, lead, distance, d_min, *, band, name):
    heads = x.shape[1]
    zeros = (0,) * 2
    return pl.pallas_call(
        functools.partial(_bias_tile_kernel, distance=distance, band=band),
        out_shape=jax.ShapeDtypeStruct((heads, *lead, T, T), F32),
        grid=(heads, *lead),
        in_specs=[pl.BlockSpec((1, 1, 1, 2 * T),
                               lambda h, *idx: (distance(h, *idx) - d_min, h, 0, 0))],
        out_specs=pl.BlockSpec((1,) * (1 + len(lead)) + (T, T), lambda h, *idx: (h, *idx, *zeros)),
        compiler_params=_cparams(*(("parallel",) * (1 + len(lead)))),
        name=name,
    )(x)


def _t5_distance(h, parity, which, half):
    return parity + 2 * (1 - which) - half


def _band_distance(h, half):
    return 1 - half


def _softmax_step(s, vt, m_ref, acc_ref):
    rows = vt.shape[0]
    m_prev = m_ref[...]
    m_new = jnp.maximum(m_prev, jnp.max(s, axis=0, keepdims=True))
    alpha = jnp.exp2(m_prev - m_new)
    p = jnp.exp2(s - m_new).astype(BF16)
    acc_ref[:rows, :] = acc_ref[:rows, :] * alpha + jnp.dot(vt, p, preferred_element_type=F32)
    m_ref[...] = m_new


def _keys(j, n_blocks=1):
    return pl.ds(pl.multiple_of(j * T, T), n_blocks * T)


def _step_keys(t):
    return _keys(2 * t, 2)


def _init(m_ref, acc_ref):
    m_ref[...] = jnp.full_like(m_ref, NEG_INF)
    acc_ref[...] = jnp.zeros_like(acc_ref)


def _attn_a_kernel(q_ref, k_ref, vt_ref, tile_ref, lp_ref, sg_ref, o_ref,
                   m0_ref, m1_ref, acc0_ref, acc1_ref, s0a_ref, s0b_ref, s1a_ref, s1b_ref,
                   *, lam_init):
    i = pl.program_id(2)
    streams = ((m0_ref, acc0_ref), (m1_ref, acc1_ref))
    buffers = ((s0a_ref, s0b_ref), (s1a_ref, s1b_ref))
    for m_ref, acc_ref in streams:
        _init(m_ref, acc_ref)
    q = q_ref[...]
    parity = i % 2
    n_steps = i // 2 + 1
    n_far = jnp.maximum(n_steps - 2, 0)

    def fill(which, t):
        for sub in range(2):
            lanes = slice(sub * LANE, (sub + 1) * LANE)
            buffers[sub][which][...] = _dot_nt(k_ref[_step_keys(t), lanes], q[:, lanes])

    def consume(which, t, near=None):
        vt = vt_ref[:, _step_keys(t)]
        for sub, (m_ref, acc_ref) in enumerate(streams):
            s = buffers[sub][which][...]
            if near is not None:
                s = s + tile_ref[0, parity, near]
            _softmax_step(s, vt, m_ref, acc_ref)

    fill(0, 0)

    def far_body(u, carry):
        fill(1, 2 * u + 1)
        consume(0, 2 * u)
        fill(0, 2 * u + 2)
        consume(1, 2 * u + 1)
        return carry

    lax.fori_loop(0, n_far // 2, far_body, 0)

    @pl.when(n_steps == 1)
    def _():
        consume(0, 0, near=1)

    @pl.when(jnp.logical_and(n_steps >= 2, n_far % 2 == 0))
    def _():
        fill(1, n_steps - 1)
        consume(0, n_steps - 2, near=0)
        consume(1, n_steps - 1, near=1)

    @pl.when(n_far % 2 == 1)
    def _():
        fill(1, n_far)
        consume(0, n_far - 1)
        fill(0, n_far + 1)
        consume(1, n_far, near=0)
        consume(0, n_far + 1, near=1)

    lp = lp_ref[...]
    lam = (jnp.exp(jnp.sum(lp[0:1] * lp[1:2], axis=-1, keepdims=True))
           - jnp.exp(jnp.sum(lp[2:3] * lp[3:4], axis=-1, keepdims=True)) + lam_init)
    acc0 = acc0_ref[...]
    acc1 = acc1_ref[...]
    ot = (acc0[:A_VDIM] / acc0[A_VDIM:A_VDIM + 1]
          - lam * (acc1[:A_VDIM] / acc1[A_VDIM:A_VDIM + 1]))
    ot = _rms(ot, sg_ref[...], axis=0) * (1.0 - lam_init)
    o_ref[...] = ot.T.astype(BF16)


def _attn_a(qk, vt, tiles, lp, sg, *, b, s, lam_init):
    nq = s // T
    wa = 2 * LANE
    return pl.pallas_call(
        functools.partial(_attn_a_kernel, lam_init=lam_init),
        out_shape=jax.ShapeDtypeStruct((b * s, A_HEADS * A_VDIM), BF16),
        grid=(b, A_HEADS, nq),
        in_specs=[
            pl.BlockSpec((T, wa), lambda bi, h, i: (bi * nq + i, h)),
            _resident((s, wa), lambda bi, h, i: (bi, A_HEADS + h)),
            _resident((VA_ROWS, s), lambda bi, h, i: (bi * A_HEADS + h, 0)),
            _resident((1, 2, 2, 2 * T, T), lambda bi, h, i: (h, 0, 0, 0, 0)),
            pl.BlockSpec((4, HEAD_DIM), lambda bi, h, i: (0, 0)),
            pl.BlockSpec((A_VDIM, 1), lambda bi, h, i: (0, 0)),
        ],
        out_specs=pl.BlockSpec((T, A_VDIM), lambda bi, h, i: (bi * nq + i, h)),
        scratch_shapes=[pltpu.VMEM((1, T), F32), pltpu.VMEM((1, T), F32),
                        pltpu.VMEM((VA_ROWS, T), F32), pltpu.VMEM((VA_ROWS, T), F32)]
        + [pltpu.VMEM((2 * T, T), F32)] * 4,
        compiler_params=_cparams("parallel", "parallel", "arbitrary"),
        name="attn_a",
    )(qk, qk, vt, tiles, lp, sg)


def _pair_output(acc_refs):
    halves = []
    for acc_ref in acc_refs:
        acc = acc_ref[...]
        halves.append(acc[:HEAD_DIM] / acc[HEAD_DIM:HEAD_DIM + 1])
    return jnp.concatenate(halves, axis=0).T.astype(BF16)


def _pair_values(vt_ref, sub, keys):
    return vt_ref[sub * VBC_ROWS:(sub + 1) * VBC_ROWS, keys]


def _attn_b_kernel(q_ref, k_ref, vt_ref, tile_ref, o_ref, m0_ref, m1_ref, acc0_ref, acc1_ref):
    i = pl.program_id(2)
    streams = ((m0_ref, acc0_ref), (m1_ref, acc1_ref))
    for m_ref, acc_ref in streams:
        _init(m_ref, acc_ref)
    q = q_ref[...]
    lane_head = lax.broadcasted_iota(jnp.int32, q.shape, 1) // HEAD_DIM
    q_heads = [jnp.where(lane_head == sub, q, jnp.zeros_like(q)) for sub in range(2)]

    def step(keys, tile_rows):
        k = k_ref[keys, :]
        for sub, (m_ref, acc_ref) in enumerate(streams):
            s = _dot_nt(k, q_heads[sub]) + tile_ref[sub, tile_rows, :]
            _softmax_step(s, _pair_values(vt_ref, sub, keys), m_ref, acc_ref)

    @pl.when(i >= 1)
    def _():
        step(_keys(i - 1, 2), slice(0, 2 * T))

    @pl.when(i == 0)
    def _():
        step(_keys(0), slice(T, 2 * T))

    o_ref[...] = _pair_output((acc0_ref, acc1_ref))


def _attn_c_kernel(q_ref, k_ref, vt_ref, o_ref, m0_ref, m1_ref, acc0_ref, acc1_ref,
                   s0a_ref, s0b_ref, s1a_ref, s1b_ref):
    i = pl.program_id(2)
    streams = ((m0_ref, acc0_ref), (m1_ref, acc1_ref))
    buffers = ((s0a_ref, s0b_ref), (s1a_ref, s1b_ref))
    for m_ref, acc_ref in streams:
        _init(m_ref, acc_ref)
    q = q_ref[...]
    n_plain = i // 2

    def fill(which, t):
        for sub in range(2):
            lanes = slice(sub * LANE, (sub + 1) * LANE)
            buffers[sub][which][...] = _dot_nt(k_ref[_step_keys(t), lanes], q[:, lanes])

    def consume(which, t, masked=False):
        if masked:
            ahead = (lax.broadcasted_iota(jnp.int32, (2 * T, T), 0)
                     - lax.broadcasted_iota(jnp.int32, (2 * T, T), 1))
            causal = ahead <= (i - 2 * t) * T
        for sub, (m_ref, acc_ref) in enumerate(streams):
            s = buffers[sub][which][...]
            if masked:
                s = jnp.where(causal, s, NEG_INF)
            _softmax_step(s, _pair_values(vt_ref, sub, _step_keys(t)), m_ref, acc_ref)

    fill(0, 0)

    def body(u, carry):
        fill(1, 2 * u + 1)
        consume(0, 2 * u)
        fill(0, 2 * u + 2)
        consume(1, 2 * u + 1)
        return carry

    lax.fori_loop(0, n_plain // 2, body, 0)

    @pl.when(n_plain % 2 == 0)
    def _():
        consume(0, n_plain, masked=True)

    @pl.when(n_plain % 2 == 1)
    def _():
        fill(1, n_plain)
        consume(0, n_plain - 1)
        consume(1, n_plain, masked=True)

    o_ref[...] = _pair_output((acc0_ref, acc1_ref))


def _attn_bc(kernel, q_arr, k_arr, vt, extra, extra_specs, *, b, s, heads, width, q_block0,
             k_block0, name, score_buffers=0):
    nq = s // T
    pairs = heads // 2
    return pl.pallas_call(
        kernel,
        out_shape=jax.ShapeDtypeStruct((b * s, heads * HEAD_DIM), BF16),
        grid=(b, pairs, nq),
        in_specs=[
            pl.BlockSpec((T, width), lambda bi, hp, i: (bi * nq + i, q_block0 + hp)),
            _resident((s, width), lambda bi, hp, i: (bi, k_block0 + hp)),
            _resident((2 * VBC_ROWS, s), lambda bi, hp, i: (bi * pairs + hp, 0)),
            *extra_specs,
        ],
        out_specs=pl.BlockSpec((T, 2 * HEAD_DIM), lambda bi, hp, i: (bi * nq + i, hp)),
        scratch_shapes=[pltpu.VMEM((1, T), F32)] * 2 + [pltpu.VMEM((VBC_ROWS, T), F32)] * 2
        + [pltpu.VMEM((2 * T, T), F32)] * score_buffers,
        compiler_params=_cparams("parallel", "parallel", "arbitrary"),
        name=name,
    )(q_arr, k_arr, vt, *extra)


def _merge_kernel(x_ref, g_ref, wg_ref, oa_ref, ob_ref, oc_ref, wa_ref, wb_ref, wc_ref,
                  wo_ref, o_ref):
    x = x_ref[...]
    d = x.shape[1]
    h = _rms(x, g_ref[...]).astype(BF16)
    gates = jax.nn.sigmoid(jnp.dot(h, wg_ref[...], preferred_element_type=F32))
    merged = None
    for n, (o_br, w_br) in enumerate(((oa_ref, wa_ref), (ob_ref, wb_ref), (oc_ref, wc_ref))):
        term = gates[:, n * d:(n + 1) * d] * jnp.dot(o_br[...], w_br[...],
                                                     preferred_element_type=F32)
        merged = term if merged is None else merged + term
    o_ref[...] = x + jnp.dot(merged.astype(BF16), wo_ref[...], preferred_element_type=F32)


def _merge(x2, g, wg, oa, ob, oc, wa, wb, wc, wo, *, tm):
    n, d = x2.shape
    row = lambda i: (i, 0)
    const = lambda i: (0, 0)
    return pl.pallas_call(
        _merge_kernel,
        out_shape=jax.ShapeDtypeStruct((n, d), F32),
        grid=(n // tm,),
        in_specs=[
            pl.BlockSpec((tm, d), row),
            pl.BlockSpec((1, d), const),
            _resident(wg.shape, const),
            pl.BlockSpec((tm, oa.shape[1]), row),
            pl.BlockSpec((tm, ob.shape[1]), row),
            pl.BlockSpec((tm, oc.shape[1]), row),
            _resident(wa.shape, const),
            _resident(wb.shape, const),
            _resident(wc.shape, const),
            _resident(wo.shape, const),
        ],
        out_specs=pl.BlockSpec((tm, d), row),
        compiler_params=_cparams("parallel"),
        name="merge",
    )(x2, g, wg, oa, ob, oc, wa, wb, wc, wo)


def _ffn_kernel(x_ref, g_ref, wg_ref, wu_ref, wd_ref, gf_ref, o_ref, *, n_chunks, final_norm):
    x = x_ref[...]
    h = _rms(x, g_ref[...]).astype(BF16)
    fc = wg_ref.shape[1] // n_chunks
    y = x
    for c in range(n_chunks):
        cols = slice(c * fc, (c + 1) * fc)
        gt = jnp.dot(h, wg_ref[:, cols], preferred_element_type=F32)
        up = jnp.dot(h, wu_ref[:, cols], preferred_element_type=F32)
        act = (gt * jax.nn.sigmoid(gt) * up).astype(BF16)
        y = y + jnp.dot(act, wd_ref[cols, :], preferred_element_type=F32)
    if final_norm:
        y = _rms(y, gf_ref[...])
    o_ref[...] = y


def _ffn(x2, g, wg, wu, wd, gf, *, tm, n_chunks, final_norm):
    n, d = x2.shape
    row = lambda i: (i, 0)
    const = lambda i: (0, 0)
    return pl.pallas_call(
        functools.partial(_ffn_kernel, n_chunks=n_chunks, final_norm=final_norm),
        out_shape=jax.ShapeDtypeStruct((n, d), F32),
        grid=(n // tm,),
        in_specs=[
            pl.BlockSpec((tm, d), row),
            pl.BlockSpec((1, d), const),
            _resident(wg.shape, const),
            _resident(wu.shape, const),
            _resident(wd.shape, const),
            pl.BlockSpec((1, d), const),
        ],
        out_specs=pl.BlockSpec((tm, d), row),
        compiler_params=_cparams("parallel"),
        name="ffn",
    )(x2, g, wg, wu, wd, gf)


def _pad_heads(w, heads, width, slot):
    d = w.shape[0]
    w = w.reshape(d, heads, width)
    return jnp.pad(w, ((0, 0), (0, 0), (0, slot - width))).reshape(d, heads * slot)


def _ones_col(heads, slot, at):
    col = np.zeros((heads, slot), np.float32)
    col[:, at] = 1.0
    return jnp.asarray(col.reshape(heads * slot, 1))


def _t5_bucket(rel):
    nb = T5_BUCKETS // 2
    max_exact = nb // 2
    n = jnp.abs(rel)
    nf = jnp.maximum(n, 1).astype(jnp.float32)
    large = max_exact + (jnp.log(nf / max_exact) / math.log(T5_MAX_DIST / max_exact)
                         * (nb - max_exact)).astype(jnp.int32)
    large = jnp.minimum(large, nb - 1)
    return jnp.where(rel > 0, nb, 0) + jnp.where(n < max_exact, n, large)


def _tile_rel(d_min, d_max):
    return ((T - 1 - jnp.arange(2 * T))[None, :]
            - jnp.arange(d_min, d_max + 1)[:, None] * T)


def _rel_vectors(table, idx):
    return jnp.moveaxis(table[idx], -1, 1).astype(F32)[:, :, None, :]


def kernel(x, norm_mix_g, w_in, b_forget, diff_lambda, diff_subln_g, t5_table, b_rel_table,
           w_br_a, w_br_b, w_br_c, w_out, norm_ffn_g, w_gate_up, w_down, final_norm_g):
    b, s, d = x.shape
    depth = w_in.shape[0]
    assert s % (2 * T) == 0 and T // CHUNK == B_LEFT_CHUNKS
    n = b * s
    scale = HEAD_DIM ** -0.5 * LOG2E
    a_qk_w = A_HEADS * 2 * HEAD_DIM
    a_v_w = A_HEADS * A_VDIM
    bw = B_HEADS * HEAD_DIM
    cw = C_HEADS * HEAD_DIM
    d_ff = w_down.shape[1]
    x2 = x.reshape(n, d)

    t5_far = t5_table[T5_BUCKETS // 2 - 1].astype(F32)
    t5_tiles = _bias_tiles(
        _rel_vectors(t5_table, _t5_bucket(_tile_rel(-1, A_NEAR))) - t5_far[None, :, None, None],
        (2, 2, 2), _t5_distance, -1, band=False, name="bias_tiles_t5",
    ).reshape(A_HEADS, 2, 2, 2 * T, T)
    far_pieces, rest = [], t5_far * LOG2E
    for _ in range(C_PARTS):
        piece = rest.astype(BF16).astype(F32)
        far_pieces.append(piece)
        rest = rest - piece
    far_pieces = jnp.repeat(jnp.stack(far_pieces, axis=-1), 2, axis=0)
    spare = slice(HEAD_DIM, HEAD_DIM + C_PARTS)
    brow_qa = jnp.zeros((2 * A_HEADS, LANE), F32).at[:, spare].set(1.0)
    brow_ka = jnp.zeros((2 * A_HEADS, LANE), F32).at[:, spare].set(far_pieces)
    brow_qc = jnp.zeros((C_HEADS, LANE), F32).at[:, HEAD_DIM + C_PARTS:HEAD_DIM + 2 * C_PARTS].set(1.0)
    brow_kc = jnp.zeros((C_HEADS, LANE), F32).at[:, spare].set(1.0)
    b_slot0 = 4 * A_HEADS
    c_slot0 = b_slot0 + B_HEADS
    brow = jnp.concatenate([brow_qa.reshape(1, -1), brow_ka.reshape(1, -1),
                            jnp.zeros((1, 2 * B_HEADS * HEAD_DIM), F32),
                            brow_qc.reshape(1, -1), brow_kc.reshape(1, -1)], axis=1)
    band_idx = jnp.clip(_tile_rel(0, 1), -B_REL_CLIP, CHUNK - 1) + B_REL_CLIP
    onea = _ones_col(A_HEADS, VA_ROWS, A_VDIM)
    onebc = _ones_col(B_HEADS, VBC_ROWS, HEAD_DIM)

    for l in range(depth):
        w = w_in[l]
        pieces, at = [], 0
        for width in (a_qk_w, a_qk_w, a_v_w, bw, bw, bw, cw, cw, cw, C_HEADS, N_BRANCH * d):
            pieces.append(w[:, at:at + width])
            at += width
        a_q, a_k, a_v, b_q, b_k, b_v, c_q, c_k, c_v, c_f, w_gates = pieces
        w_qk = jnp.concatenate([
            _pad_heads(a_q * scale, 2 * A_HEADS, HEAD_DIM, LANE),
            _pad_heads(a_k, 2 * A_HEADS, HEAD_DIM, LANE),
            b_q * scale,
            b_k,
            _pad_heads(c_q * scale, C_HEADS, HEAD_DIM, LANE),
            _pad_heads(c_k, C_HEADS, HEAD_DIM, LANE),
        ], axis=1).astype(BF16)
        wf = jnp.pad(c_f, ((0, 0), (0, LANE - C_HEADS))).astype(BF16)
        wva = _pad_heads(a_v, A_HEADS, A_VDIM, VA_ROWS).T.astype(BF16)
        wvb = _pad_heads(b_v, B_HEADS, HEAD_DIM, VBC_ROWS).T.astype(BF16)
        wvc = _pad_heads(c_v, C_HEADS, HEAD_DIM, VBC_ROWS).T.astype(BF16)

        bf_row = jnp.pad(b_forget[l].astype(F32), (0, LANE - C_HEADS))[None]
        qk, vta, vtb, vtc = _in_proj(
            x2, norm_mix_g[l][None], w_qk, brow, wf, bf_row, wva, wvb, wvc, onea, onebc,
            b=b, tm=min(1024, s), cq_tile=c_slot0 // C_HEADS)

        lam_init = 0.8 - 0.6 * math.exp(-0.3 * l)
        o_a = _attn_a(qk, vta, t5_tiles, diff_lambda[l].astype(F32),
                      diff_subln_g[l][:, None].astype(F32), b=b, s=s, lam_init=lam_init)

        band_tiles = _bias_tiles(
            _rel_vectors(b_rel_table[l], band_idx), (2,), _band_distance, 0, band=True,
            name="bias_tiles_band").reshape(B_HEADS, 2 * T, T)
        o_b = _attn_bc(
            _attn_b_kernel, qk, qk, vtb, (band_tiles,),
            (_resident((2, 2 * T, T), lambda bi, hp, i: (hp, 0, 0)),),
            b=b, s=s, heads=B_HEADS, width=LANE, q_block0=b_slot0,
            k_block0=b_slot0 + B_HEADS // 2, name="attn_b")

        o_c = _attn_bc(_attn_c_kernel, qk, qk, vtc, (), (), b=b, s=s, heads=C_HEADS,
                       width=2 * LANE, q_block0=c_slot0 // 2, k_block0=(c_slot0 + C_HEADS) // 2,
                       name="attn_c", score_buffers=4)

        x2 = _merge(
            x2, norm_mix_g[l][None], w_gates.astype(BF16), o_a, o_b, o_c,
            w_br_a[l].astype(BF16), w_br_b[l].astype(BF16), w_br_c[l].astype(BF16),
            w_out[l].astype(BF16), tm=min(512, n))

        x2 = _ffn(
            x2, norm_ffn_g[l][None], w_gate_up[l][:, :d_ff].astype(BF16),
            w_gate_up[l][:, d_ff:].astype(BF16), w_down[l].astype(BF16),
            final_norm_g[None], tm=min(512, n), n_chunks=2, final_norm=(l == depth - 1))

    return x2.reshape(b, s, d)
```

```python
import functools
import math

import numpy as np
import jax
import jax.numpy as jnp
from jax import lax
from jax.experimental import pallas as pl
from jax.experimental.pallas import tpu as pltpu

CHUNK = 64
HEAD_DIM = 64
A_HEADS = 4
A_VDIM = 2 * HEAD_DIM
B_HEADS = 8
B_LEFT_CHUNKS = 8
B_REL_CLIP = 128
C_HEADS = 8
T5_BUCKETS = 32
T5_MAX_DIST = 2048
N_BRANCH = 3
RMS_EPS = 1e-6
NEG_INF = -1e30
LOG2E = math.log2(math.e)

LANE = 128
BF16_ROWS = 16
T = 512
A_NEAR = 3
VA_ROWS = A_VDIM + BF16_ROWS
VBC_ROWS = HEAD_DIM + BF16_ROWS
C_PARTS = 3
VMEM_LIMIT = 56 * 1024 * 1024

F32 = jnp.float32
BF16 = jnp.bfloat16


def _cparams(*sem):
    return pltpu.CompilerParams(dimension_semantics=sem, vmem_limit_bytes=VMEM_LIMIT)


def _resident(block_shape, index_map):
    return pl.BlockSpec(block_shape, index_map, pipeline_mode=pl.Buffered(1))


def _rms(x, g, axis=-1):
    y = x * lax.rsqrt(jnp.mean(x * x, axis=axis, keepdims=True) + RMS_EPS)
    return y * g


def _dot_nt(a, b):
    return lax.dot_general(a, b, (((1,), (1,)), ((), ())), preferred_element_type=F32)


F_LANE0 = HEAD_DIM + C_HEADS


def _bf16_pieces(x):
    pieces, rest = [], x
    for _ in range(C_PARTS):
        piece = rest.astype(BF16).astype(F32)
        pieces.append(piece)
        rest = rest - piece
    return pieces


def _in_proj_kernel(x_ref, g_ref, w_ref, brow_ref, bf_ref, tri_ref, pq_ref, pk_ref,
                    wva_ref, wvb_ref, wvc_ref, onea_ref, onebc_ref,
                    o_ref, va_ref, vb_ref, vc_ref, carry_ref, *, row_tiles_per_seq, cq_tile):
    h = _rms(x_ref[...], g_ref[...]).astype(BF16)
    va_ref[...] = (_dot_nt(wva_ref[...], h) + onea_ref[...]).astype(BF16)
    vb_ref[...] = (_dot_nt(wvb_ref[...], h) + onebc_ref[...]).astype(BF16)
    vc_ref[...] = (_dot_nt(wvc_ref[...], h) + onebc_ref[...]).astype(BF16)
    tn = pq_ref.shape[1]

    def tile(j):
        cols = slice(j * tn, (j + 1) * tn)
        return jnp.dot(h, w_ref[:, cols], preferred_element_type=F32) + brow_ref[:, cols]

    first = tile(0)
    o_ref[:, :tn] = first.astype(BF16)

    @pl.when(pl.program_id(0) % row_tiles_per_seq == 0)
    def _():
        carry_ref[...] = jnp.zeros_like(carry_ref)

    x = first[:, :LANE] + bf_ref[...]
    log_f = -(jnp.maximum(-x, 0.0) + jnp.log1p(jnp.exp(-jnp.abs(x))))
    pieces = jnp.concatenate([p.astype(BF16) for p in _bf16_pieces(log_f)], axis=1)
    sums = jnp.dot(tri_ref[...], pieces, preferred_element_type=F32)
    c = carry_ref[...]
    for p in range(C_PARTS):
        c = c + sums[:, p * LANE:(p + 1) * LANE]
    carry_ref[...] = c[c.shape[0] - 1:, :]

    is_gate = lax.broadcasted_iota(jnp.int32, c.shape, 1) // C_HEADS == F_LANE0 // C_HEADS
    packed = None
    for p, piece in enumerate(_bf16_pieces(c * LOG2E)):
        piece = jnp.where(is_gate, piece, 0.0)
        piece = piece if p == 0 else pltpu.roll(piece, p * C_HEADS, 1)
        packed = piece if packed is None else packed + piece
    packed = packed.astype(BF16)

    for j in range(1, w_ref.shape[1] // tn):
        acc = tile(j)
        if j == cq_tile:
            acc = acc + jnp.dot(packed, pq_ref[...], preferred_element_type=F32)
        if j == cq_tile + 1:
            acc = acc - jnp.dot(packed, pk_ref[...], preferred_element_type=F32)
        o_ref[:, j * tn:(j + 1) * tn] = acc.astype(BF16)


def _placement_constants():
    w = C_HEADS * LANE
    pq = np.zeros((LANE, w), np.float32)
    pk = np.zeros((LANE, w), np.float32)
    for h in range(C_HEADS):
        for p in range(C_PARTS):
            pq[F_LANE0 + p * C_HEADS + h, h * LANE + HEAD_DIM + p] = 1.0
            pk[F_LANE0 + p * C_HEADS + h, h * LANE + HEAD_DIM + C_PARTS + p] = 1.0
    return jnp.asarray(pq, BF16), jnp.asarray(pk, BF16)


def _in_proj(x2, g, w_qk, brow, bf_row, wva, wvb, wvc, onea, onebc, *, b, tm, cq_tile):
    n, d = x2.shape
    s = n // b
    nt = s // tm
    ncol = w_qk.shape[1]
    ra, rbc = wva.shape[0], wvb.shape[0]
    tri = jnp.asarray(np.tril(np.ones((tm, tm), np.float32)), BF16)
    pq, pk = _placement_constants()
    const = lambda i: (0, 0)
    vt_map = lambda i: (i // nt, i % nt)
    return pl.pallas_call(
        functools.partial(_in_proj_kernel, row_tiles_per_seq=nt, cq_tile=cq_tile),
        out_shape=(jax.ShapeDtypeStruct((n, ncol), BF16),
                   jax.ShapeDtypeStruct((b * ra, s), BF16), jax.ShapeDtypeStruct((b * rbc, s), BF16),
                   jax.ShapeDtypeStruct((b * rbc, s), BF16)),
        grid=(n // tm,),
        in_specs=[
            pl.BlockSpec((tm, d), lambda i: (i, 0)),
            pl.BlockSpec((1, d), const),
            _resident((d, ncol), const),
            pl.BlockSpec((1, ncol), const),
            pl.BlockSpec((1, LANE), const),
            _resident((tm, tm), const),
            _resident(pq.shape, const),
            _resident(pk.shape, const),
            _resident((ra, d), const),
            _resident((rbc, d), const),
            _resident((rbc, d), const),
            pl.BlockSpec((ra, 1), const),
            pl.BlockSpec((rbc, 1), const),
        ],
        out_specs=(
            pl.BlockSpec((tm, ncol), lambda i: (i, 0)),
            pl.BlockSpec((ra, tm), vt_map),
            pl.BlockSpec((rbc, tm), vt_map),
            pl.BlockSpec((rbc, tm), vt_map),
        ),
        scratch_shapes=[pltpu.VMEM((1, LANE), F32)],
        compiler_params=_cparams("arbitrary"),
        name="in_proj",
    )(x2, g, w_qk, brow, bf_row, tri, pq, pk, wva, wvb, wvc, onea, onebc)


def _bias_tile_kernel(x_ref, o_ref, *, distance, band):
    d = distance(*(pl.program_id(a) for a in range(len(o_ref.shape) - 2)))
    x = jnp.broadcast_to(x_ref[0, 0], (T, 2 * T))
    tile = pltpu.roll(x, T + 1, 1, stride=1, stride_axis=0)[:, :T] * LOG2E
    ahead = (lax.broadcasted_iota(jnp.int32, (T, T), 0) // CHUNK
             - lax.broadcasted_iota(jnp.int32, (T, T), 1) // CHUNK
             - d * (T // CHUNK))
    if band:
        tile = jnp.where(ahead >= -B_LEFT_CHUNKS, tile, NEG_INF)
    o_ref[...] = jnp.where(ahead <= 0, tile, NEG_INF).reshape(o_ref.shape)


def _bias_tiles(x, lead, distance, d_min, *, band, name):
    heads = x.shape[1]
    zeros = (0,) * 2
    return pl.pallas_call(
        functools.partial(_bias_tile_kernel, distance=distance, band=band),
        out_shape=jax.ShapeDtypeStruct((heads, *lead, T, T), F32),
        grid=(heads, *lead),
        in_specs=[pl.BlockSpec((1, 1, 1, 2 * T),
                               lambda h, *idx: (distance(h, *idx) - d_min, h, 0, 0))],
        out_specs=pl.BlockSpec((1,) * (1 + len(lead)) + (T, T), lambda h, *idx: (h, *idx, *zeros)),
        compiler_params=_cparams(*(("parallel",) * (1 + len(lead)))),
        name=name,
    )(x)


def _t5_distance(h, parity, which, half):
    return parity + 2 * (1 - which) - half


def _band_distance(h, half):
    return 1 - half


def _softmax_step(s, vt, m_ref, acc_ref, col_max=None):
    rows = vt.shape[0]
    m_prev = m_ref[...]
    if col_max is None:
        col_max = jnp.max(s, axis=0, keepdims=True)
    m_new = jnp.maximum(m_prev, col_max)
    alpha = jnp.exp2(m_prev - m_new)
    p = jnp.exp2(s - m_new).astype(BF16)
    acc_ref[:rows, :] = acc_ref[:rows, :] * alpha + jnp.dot(vt, p, preferred_element_type=F32)
    m_ref[...] = m_new


def _keys(j, n_blocks=1):
    return pl.ds(pl.multiple_of(j * T, T), n_blocks * T)


def _step_keys(t):
    return _keys(2 * t, 2)


def _init(m_ref, acc_ref):
    m_ref[...] = jnp.full_like(m_ref, NEG_INF)
    acc_ref[...] = jnp.zeros_like(acc_ref)


def _attn_a_kernel(q_ref, k_ref, vt_ref, tile_ref, lp_ref, sg_ref, o_ref,
                   m0_ref, m1_ref, acc0_ref, acc1_ref, s0a_ref, s0b_ref, s1a_ref, s1b_ref,
                   smax_ref, *, lam_init):
    i = pl.program_id(2)
    streams = ((m0_ref, acc0_ref), (m1_ref, acc1_ref))
    buffers = ((s0a_ref, s0b_ref), (s1a_ref, s1b_ref))
    for m_ref, acc_ref in streams:
        _init(m_ref, acc_ref)
    q = q_ref[...]
    parity = i % 2
    n_steps = i // 2 + 1
    n_far = jnp.maximum(n_steps - 2, 0)

    def fill(which, t):
        for sub in range(2):
            lanes = slice(sub * LANE, (sub + 1) * LANE)
            s = _dot_nt(k_ref[_step_keys(t), lanes], q[:, lanes])
            buffers[sub][which][...] = s
            smax_ref[2 * sub + which] = jnp.max(s, axis=0, keepdims=True)

    def consume(which, t, near=None):
        vt = vt_ref[:, _step_keys(t)]
        for sub, (m_ref, acc_ref) in enumerate(streams):
            s = buffers[sub][which][...]
            col_max = smax_ref[2 * sub + which]
            if near is not None:
                s = s + tile_ref[0, parity, near]
                col_max = None
            _softmax_step(s, vt, m_ref, acc_ref, col_max)

    fill(0, 0)

    def far_body(u, carry):
        fill(1, 2 * u + 1)
        consume(0, 2 * u)
        fill(0, 2 * u + 2)
        consume(1, 2 * u + 1)
        return carry

    lax.fori_loop(0, n_far // 2, far_body, 0)

    @pl.when(n_steps == 1)
    def _():
        consume(0, 0, near=1)

    @pl.when(jnp.logical_and(n_steps >= 2, n_far % 2 == 0))
    def _():
        fill(1, n_steps - 1)
        consume(0, n_steps - 2, near=0)
        consume(1, n_steps - 1, near=1)

    @pl.when(n_far % 2 == 1)
    def _():
        fill(1, n_far)
        consume(0, n_far - 1)
        fill(0, n_far + 1)
        consume(1, n_far, near=0)
        consume(0, n_far + 1, near=1)

    lp = lp_ref[...]
    lam = (jnp.exp(jnp.sum(lp[0:1] * lp[1:2], axis=-1, keepdims=True))
           - jnp.exp(jnp.sum(lp[2:3] * lp[3:4], axis=-1, keepdims=True)) + lam_init)
    acc0 = acc0_ref[...]
    acc1 = acc1_ref[...]
    ot = (acc0[:A_VDIM] / acc0[A_VDIM:A_VDIM + 1]
          - lam * (acc1[:A_VDIM] / acc1[A_VDIM:A_VDIM + 1]))
    ot = _rms(ot, sg_ref[...], axis=0) * (1.0 - lam_init)
    o_ref[...] = ot.T.astype(BF16)


def _attn_a(qk, vt, tiles, lp, sg, *, b, s, lam_init):
    nq = s // T
    wa = 2 * LANE
    return pl.pallas_call(
        functools.partial(_attn_a_kernel, lam_init=lam_init),
        out_shape=jax.ShapeDtypeStruct((b * s, A_HEADS * A_VDIM), BF16),
        grid=(b, A_HEADS, nq),
        in_specs=[
            pl.BlockSpec((T, wa), lambda bi, h, i: (bi * nq + i, h)),
            _resident((s, wa), lambda bi, h, i: (bi, A_HEADS + h)),
            _resident((VA_ROWS, s), lambda bi, h, i: (bi * A_HEADS + h, 0)),
            _resident((1, 2, 2, 2 * T, T), lambda bi, h, i: (h, 0, 0, 0, 0)),
            pl.BlockSpec((4, HEAD_DIM), lambda bi, h, i: (0, 0)),
            pl.BlockSpec((A_VDIM, 1), lambda bi, h, i: (0, 0)),
        ],
        out_specs=pl.BlockSpec((T, A_VDIM), lambda bi, h, i: (bi * nq + i, h)),
        scratch_shapes=[pltpu.VMEM((1, T), F32), pltpu.VMEM((1, T), F32),
                        pltpu.VMEM((VA_ROWS, T), F32), pltpu.VMEM((VA_ROWS, T), F32)]
        + [pltpu.VMEM((2 * T, T), F32)] * 4 + [pltpu.VMEM((4, 1, T), F32)],
        compiler_params=_cparams("parallel", "parallel", "arbitrary"),
        name="attn_a",
    )(qk, qk, vt, tiles, lp, sg)


def _pair_output(acc_refs):
    halves = []
    for acc_ref in acc_refs:
        acc = acc_ref[...]
        halves.append(acc[:HEAD_DIM] / acc[HEAD_DIM:HEAD_DIM + 1])
    return jnp.concatenate(halves, axis=0).T.astype(BF16)


def _pair_values(vt_ref, sub, keys):
    return vt_ref[sub * VBC_ROWS:(sub + 1) * VBC_ROWS, keys]


def _attn_b_kernel(q_ref, k_ref, vt_ref, tile_ref, o_ref, m0_ref, m1_ref, acc0_ref, acc1_ref):
    i = pl.program_id(2)
    streams = ((m0_ref, acc0_ref), (m1_ref, acc1_ref))
    for m_ref, acc_ref in streams:
        _init(m_ref, acc_ref)
    q = q_ref[...]
    lane_head = lax.broadcasted_iota(jnp.int32, q.shape, 1) // HEAD_DIM
    q_heads = [jnp.where(lane_head == sub, q, jnp.zeros_like(q)) for sub in range(2)]

    def step(keys, tile_rows):
        k = k_ref[keys, :]
        for sub, (m_ref, acc_ref) in enumerate(streams):
            s = _dot_nt(k, q_heads[sub]) + tile_ref[sub, tile_rows, :]
            _softmax_step(s, _pair_values(vt_ref, sub, keys), m_ref, acc_ref)

    @pl.when(i >= 1)
    def _():
        step(_keys(i - 1, 2), slice(0, 2 * T))

    @pl.when(i == 0)
    def _():
        step(_keys(0), slice(T, 2 * T))

    o_ref[...] = _pair_output((acc0_ref, acc1_ref))


def _attn_c_kernel(q_ref, k_ref, vt_ref, o_ref, m0_ref, m1_ref, acc0_ref, acc1_ref,
                   s0a_ref, s0b_ref, s1a_ref, s1b_ref, smax_ref):
    i = pl.program_id(2)
    streams = ((m0_ref, acc0_ref), (m1_ref, acc1_ref))
    buffers = ((s0a_ref, s0b_ref), (s1a_ref, s1b_ref))
    for m_ref, acc_ref in streams:
        _init(m_ref, acc_ref)
    q = q_ref[...]
    n_plain = i // 2

    def fill(which, t):
        for sub in range(2):
            lanes = slice(sub * LANE, (sub + 1) * LANE)
            s = _dot_nt(k_ref[_step_keys(t), lanes], q[:, lanes])
            buffers[sub][which][...] = s
            smax_ref[2 * sub + which] = jnp.max(s, axis=0, keepdims=True)

    def consume(which, t, masked=False):
        if masked:
            ahead = (lax.broadcasted_iota(jnp.int32, (2 * T, T), 0)
                     - lax.broadcasted_iota(jnp.int32, (2 * T, T), 1))
            causal = ahead <= (i - 2 * t) * T
        for sub, (m_ref, acc_ref) in enumerate(streams):
            s = buffers[sub][which][...]
            col_max = smax_ref[2 * sub + which]
            if masked:
                s = jnp.where(causal, s, NEG_INF)
                col_max = None
            _softmax_step(s, _pair_values(vt_ref, sub, _step_keys(t)), m_ref, acc_ref, col_max)

    fill(0, 0)

    def body(u, carry):
        fill(1, 2 * u + 1)
        consume(0, 2 * u)
        fill(0, 2 * u + 2)
        consume(1, 2 * u + 1)
        return carry

    lax.fori_loop(0, n_plain // 2, body, 0)

    @pl.when(n_plain % 2 == 0)
    def _():
        consume(0, n_plain, masked=True)

    @pl.when(n_plain % 2 == 1)
    def _():
        fill(1, n_plain)
        consume(0, n_plain - 1)
        consume(1, n_plain, masked=True)

    o_ref[...] = _pair_output((acc0_ref, acc1_ref))


def _attn_bc(kernel, q_arr, k_arr, vt, extra, extra_specs, *, b, s, heads, width, q_block0,
             k_block0, name, score_buffers=0):
    nq = s // T
    pairs = heads // 2
    return pl.pallas_call(
        kernel,
        out_shape=jax.ShapeDtypeStruct((b * s, heads * HEAD_DIM), BF16),
        grid=(b, pairs, nq),
        in_specs=[
            pl.BlockSpec((T, width), lambda bi, hp, i: (bi * nq + i, q_block0 + hp)),
            _resident((s, width), lambda bi, hp, i: (bi, k_block0 + hp)),
            _resident((2 * VBC_ROWS, s), lambda bi, hp, i: (bi * pairs + hp, 0)),
            *extra_specs,
        ],
        out_specs=pl.BlockSpec((T, 2 * HEAD_DIM), lambda bi, hp, i: (bi * nq + i, hp)),
        scratch_shapes=[pltpu.VMEM((1, T), F32)] * 2 + [pltpu.VMEM((VBC_ROWS, T), F32)] * 2
        + [pltpu.VMEM((2 * T, T), F32)] * score_buffers
        + [pltpu.VMEM((score_buffers, 1, T), F32)] * (score_buffers > 0),
        compiler_params=_cparams("parallel", "parallel", "arbitrary"),
        name=name,
    )(q_arr, k_arr, vt, *extra)


def _merge_kernel(x_ref, g_ref, wg_ref, oa_ref, ob_ref, oc_ref, wa_ref, wb_ref, wc_ref,
                  wo_ref, o_ref):
    x = x_ref[...]
    d = x.shape[1]
    h = _rms(x, g_ref[...]).astype(BF16)
    gates = jax.nn.sigmoid(jnp.dot(h, wg_ref[...], preferred_element_type=F32))
    merged = None
    for n, (o_br, w_br) in enumerate(((oa_ref, wa_ref), (ob_ref, wb_ref), (oc_ref, wc_ref))):
        term = gates[:, n * d:(n + 1) * d] * jnp.dot(o_br[...], w_br[...],
                                                     preferred_element_type=F32)
        merged = term if merged is None else merged + term
    o_ref[...] = x + jnp.dot(merged.astype(BF16), wo_ref[...], preferred_element_type=F32)


def _merge(x2, g, wg, oa, ob, oc, wa, wb, wc, wo, *, tm):
    n, d = x2.shape
    row = lambda i: (i, 0)
    const = lambda i: (0, 0)
    return pl.pallas_call(
        _merge_kernel,
        out_shape=jax.ShapeDtypeStruct((n, d), F32),
        grid=(n // tm,),
        in_specs=[
            pl.BlockSpec((tm, d), row),
            pl.BlockSpec((1, d), const),
            _resident(wg.shape, const),
            pl.BlockSpec((tm, oa.shape[1]), row),
            pl.BlockSpec((tm, ob.shape[1]), row),
            pl.BlockSpec((tm, oc.shape[1]), row),
            _resident(wa.shape, const),
            _resident(wb.shape, const),
            _resident(wc.shape, const),
            _resident(wo.shape, const),
        ],
        out_specs=pl.BlockSpec((tm, d), row),
        compiler_params=_cparams("parallel"),
        name="merge",
    )(x2, g, wg, oa, ob, oc, wa, wb, wc, wo)


def _ffn_kernel(x_ref, g_ref, wg_ref, wu_ref, wd_ref, gf_ref, o_ref, *, n_chunks, final_norm):
    x = x_ref[...]
    h = _rms(x, g_ref[...]).astype(BF16)
    fc = wg_ref.shape[1] // n_chunks
    y = x
    for c in range(n_chunks):
        cols = slice(c * fc, (c + 1) * fc)
        gt = jnp.dot(h, wg_ref[:, cols], preferred_element_type=F32)
        up = jnp.dot(h, wu_ref[:, cols], preferred_element_type=F32)
        act = (gt * jax.nn.sigmoid(gt) * up).astype(BF16)
        y = y + jnp.dot(act, wd_ref[cols, :], preferred_element_type=F32)
    if final_norm:
        y = _rms(y, gf_ref[...])
    o_ref[...] = y


def _ffn(x2, g, wg, wu, wd, gf, *, tm, n_chunks, final_norm):
    n, d = x2.shape
    row = lambda i: (i, 0)
    const = lambda i: (0, 0)
    return pl.pallas_call(
        functools.partial(_ffn_kernel, n_chunks=n_chunks, final_norm=final_norm),
        out_shape=jax.ShapeDtypeStruct((n, d), F32),
        grid=(n // tm,),
        in_specs=[
            pl.BlockSpec((tm, d), row),
            pl.BlockSpec((1, d), const),
            _resident(wg.shape, const),
            _resident(wu.shape, const),
            _resident(wd.shape, const),
            pl.BlockSpec((1, d), const),
        ],
        out_specs=pl.BlockSpec((tm, d), row),
        compiler_params=_cparams("parallel"),
        name="ffn",
    )(x2, g, wg, wu, wd, gf)


def _pad_heads(w, heads, width, slot):
    d = w.shape[0]
    w = w.reshape(d, heads, width)
    return jnp.pad(w, ((0, 0), (0, 0), (0, slot - width))).reshape(d, heads * slot)


def _ones_col(heads, slot, at):
    col = np.zeros((heads, slot), np.float32)
    col[:, at] = 1.0
    return jnp.asarray(col.reshape(heads * slot, 1))


def _t5_bucket(rel):
    nb = T5_BUCKETS // 2
    max_exact = nb // 2
    n = jnp.abs(rel)
    nf = jnp.maximum(n, 1).astype(jnp.float32)
    large = max_exact + (jnp.log(nf / max_exact) / math.log(T5_MAX_DIST / max_exact)
                         * (nb - max_exact)).astype(jnp.int32)
    large = jnp.minimum(large, nb - 1)
    return jnp.where(rel > 0, nb, 0) + jnp.where(n < max_exact, n, large)


def _tile_rel(d_min, d_max):
    return ((T - 1 - jnp.arange(2 * T))[None, :]
            - jnp.arange(d_min, d_max + 1)[:, None] * T)


def _rel_vectors(table, idx):
    return jnp.moveaxis(table[idx], -1, 1).astype(F32)[:, :, None, :]


def kernel(x, norm_mix_g, w_in, b_forget, diff_lambda, diff_subln_g, t5_table, b_rel_table,
           w_br_a, w_br_b, w_br_c, w_out, norm_ffn_g, w_gate_up, w_down, final_norm_g):
    b, s, d = x.shape
    depth = w_in.shape[0]
    assert s % (2 * T) == 0 and T // CHUNK == B_LEFT_CHUNKS
    n = b * s
    scale = HEAD_DIM ** -0.5 * LOG2E
    a_qk_w = A_HEADS * 2 * HEAD_DIM
    a_v_w = A_HEADS * A_VDIM
    bw = B_HEADS * HEAD_DIM
    cw = C_HEADS * HEAD_DIM
    d_ff = w_down.shape[1]
    x2 = x.reshape(n, d)

    t5_far = t5_table[T5_BUCKETS // 2 - 1].astype(F32)
    t5_tiles = _bias_tiles(
        _rel_vectors(t5_table, _t5_bucket(_tile_rel(-1, A_NEAR))) - t5_far[None, :, None, None],
        (2, 2, 2), _t5_distance, -1, band=False, name="bias_tiles_t5",
    ).reshape(A_HEADS, 2, 2, 2 * T, T)
    far_pieces, rest = [], t5_far * LOG2E
    for _ in range(C_PARTS):
        piece = rest.astype(BF16).astype(F32)
        far_pieces.append(piece)
        rest = rest - piece
    far_pieces = jnp.repeat(jnp.stack(far_pieces, axis=-1), 2, axis=0)
    spare = slice(HEAD_DIM, HEAD_DIM + C_PARTS)
    brow_qa = jnp.zeros((2 * A_HEADS, LANE), F32).at[:, spare].set(1.0)
    brow_ka = jnp.zeros((2 * A_HEADS, LANE), F32).at[:, spare].set(far_pieces)
    brow_qc = jnp.zeros((C_HEADS, LANE), F32).at[:, HEAD_DIM + C_PARTS:HEAD_DIM + 2 * C_PARTS].set(1.0)
    brow_kc = jnp.zeros((C_HEADS, LANE), F32).at[:, spare].set(1.0)
    b_slot0 = 4 * A_HEADS
    c_slot0 = b_slot0 + B_HEADS
    brow = jnp.concatenate([brow_qa.reshape(1, -1), brow_ka.reshape(1, -1),
                            jnp.zeros((1, 2 * B_HEADS * HEAD_DIM), F32),
                            brow_qc.reshape(1, -1), brow_kc.reshape(1, -1)], axis=1)
    band_idx = jnp.clip(_tile_rel(0, 1), -B_REL_CLIP, CHUNK - 1) + B_REL_CLIP
    onea = _ones_col(A_HEADS, VA_ROWS, A_VDIM)
    onebc = _ones_col(B_HEADS, VBC_ROWS, HEAD_DIM)

    for l in range(depth):
        w = w_in[l]
        pieces, at = [], 0
        for width in (a_qk_w, a_qk_w, a_v_w, bw, bw, bw, cw, cw, cw, C_HEADS, N_BRANCH * d):
            pieces.append(w[:, at:at + width])
            at += width
        a_q, a_k, a_v, b_q, b_k, b_v, c_q, c_k, c_v, c_f, w_gates = pieces
        w_qk = jnp.concatenate([
            _pad_heads(a_q * scale, 2 * A_HEADS, HEAD_DIM, LANE),
            _pad_heads(a_k, 2 * A_HEADS, HEAD_DIM, LANE),
            b_q * scale,
            b_k,
            _pad_heads(c_q * scale, C_HEADS, HEAD_DIM, LANE),
            _pad_heads(c_k, C_HEADS, HEAD_DIM, LANE),
        ], axis=1).at[:, F_LANE0:F_LANE0 + C_HEADS].set(c_f).astype(BF16)
        wva = _pad_heads(a_v, A_HEADS, A_VDIM, VA_ROWS).T.astype(BF16)
        wvb = _pad_heads(b_v, B_HEADS, HEAD_DIM, VBC_ROWS).T.astype(BF16)
        wvc = _pad_heads(c_v, C_HEADS, HEAD_DIM, VBC_ROWS).T.astype(BF16)

        bf_row = jnp.zeros((1, LANE), F32).at[0, F_LANE0:F_LANE0 + C_HEADS].set(b_forget[l])
        qk, vta, vtb, vtc = _in_proj(
            x2, norm_mix_g[l][None], w_qk, brow, bf_row, wva, wvb, wvc, onea, onebc,
            b=b, tm=min(512, s), cq_tile=c_slot0 // C_HEADS)

        lam_init = 0.8 - 0.6 * math.exp(-0.3 * l)
        o_a = _attn_a(qk, vta, t5_tiles, diff_lambda[l].astype(F32),
                      diff_subln_g[l][:, None].astype(F32), b=b, s=s, lam_init=lam_init)

        band_tiles = _bias_tiles(
            _rel_vectors(b_rel_table[l], band_idx), (2,), _band_distance, 0, band=True,
            name="bias_tiles_band").reshape(B_HEADS, 2 * T, T)
        o_b = _attn_bc(
            _attn_b_kernel, qk, qk, vtb, (band_tiles,),
            (_resident((2, 2 * T, T), lambda bi, hp, i: (hp, 0, 0)),),
            b=b, s=s, heads=B_HEADS, width=LANE, q_block0=b_slot0,
            k_block0=b_slot0 + B_HEADS // 2, name="attn_b")

        o_c = _attn_bc(_attn_c_kernel, qk, qk, vtc, (), (), b=b, s=s, heads=C_HEADS,
                       width=2 * LANE, q_block0=c_slot0 // 2, k_block0=(c_slot0 + C_HEADS) // 2,
                       name="attn_c", score_buffers=4)

        x2 = _merge(
            x2, norm_mix_g[l][None], w_gates.astype(BF16), o_a, o_b, o_c,
            w_br_a[l].astype(BF16), w_br_b[l].astype(BF16), w_br_c[l].astype(BF16),
            w_out[l].astype(BF16), tm=min(512, n))

        x2 = _ffn(
            x2, norm_ffn_g[l][None], w_gate_up[l][:, :d_ff].astype(BF16),
            w_gate_up[l][:, d_ff:].astype(BF16), w_down[l].astype(BF16),
            final_norm_g[None], tm=min(512, n), n_chunks=2, final_norm=(l == depth - 1))

    return x2.reshape(b, s, d)
```

```python
import functools
import math

import numpy as np
import jax
import jax.numpy as jnp
from jax import lax
from jax.experimental import pallas as pl
from jax.experimental.pallas import tpu as pltpu

CHUNK = 64
HEAD_DIM = 64
A_HEADS = 4
A_VDIM = 2 * HEAD_DIM
B_HEADS = 8
B_LEFT_CHUNKS = 8
B_REL_CLIP = 128
C_HEADS = 8
T5_BUCKETS = 32
T5_MAX_DIST = 2048
N_BRANCH = 3
RMS_EPS = 1e-6
NEG_INF = -1e30
LOG2E = math.log2(math.e)

LANE = 128
BF16_ROWS = 16
T = 512
A_NEAR = 3
VA_ROWS = A_VDIM + BF16_ROWS
VBC_ROWS = HEAD_DIM + BF16_ROWS
C_PARTS = 3
VMEM_LIMIT = 56 * 1024 * 1024

F32 = jnp.float32
BF16 = jnp.bfloat16


def _cparams(*sem):
    return pltpu.CompilerParams(dimension_semantics=sem, vmem_limit_bytes=VMEM_LIMIT)


def _resident(block_shape, index_map):
    return pl.BlockSpec(block_shape, index_map, pipeline_mode=pl.Buffered(1))


def _rms(x, g, axis=-1):
    y = x * lax.rsqrt(jnp.mean(x * x, axis=axis, keepdims=True) + RMS_EPS)
    return y * g


def _dot_nt(a, b):
    return lax.dot_general(a, b, (((1,), (1,)), ((), ())), preferred_element_type=F32)


F_LANE0 = HEAD_DIM + C_HEADS


def _bf16_pieces(x):
    pieces, rest = [], x
    for _ in range(C_PARTS):
        piece = rest.astype(BF16).astype(F32)
        pieces.append(piece)
        rest = rest - piece
    return pieces


def _in_proj_kernel(x_ref, g_ref, w_ref, brow_ref, bf_ref, tri_ref, pq_ref, pk_ref,
                    wva_ref, wvb_ref, wvc_ref, onea_ref, onebc_ref,
                    o_ref, va_ref, vb_ref, vc_ref, carry_ref, *, row_tiles_per_seq, cq_tile):
    h = _rms(x_ref[...], g_ref[...]).astype(BF16)
    va_ref[...] = (_dot_nt(wva_ref[...], h) + onea_ref[...]).astype(BF16)
    vb_ref[...] = (_dot_nt(wvb_ref[...], h) + onebc_ref[...]).astype(BF16)
    vc_ref[...] = (_dot_nt(wvc_ref[...], h) + onebc_ref[...]).astype(BF16)
    tn = pq_ref.shape[1]

    def tile(j):
        cols = slice(j * tn, (j + 1) * tn)
        return jnp.dot(h, w_ref[:, cols], preferred_element_type=F32) + brow_ref[:, cols]

    first = tile(0)
    o_ref[:, :tn] = first.astype(BF16)

    @pl.when(pl.program_id(0) % row_tiles_per_seq == 0)
    def _():
        carry_ref[...] = jnp.zeros_like(carry_ref)

    x = first[:, :LANE] + bf_ref[...]
    log_f = -(jnp.maximum(-x, 0.0) + jnp.log1p(jnp.exp(-jnp.abs(x))))
    pieces = jnp.concatenate([p.astype(BF16) for p in _bf16_pieces(log_f)], axis=1)
    sums = jnp.dot(tri_ref[...], pieces, preferred_element_type=F32)
    c = carry_ref[...]
    for p in range(C_PARTS):
        c = c + sums[:, p * LANE:(p + 1) * LANE]
    carry_ref[...] = c[c.shape[0] - 1:, :]

    is_gate = lax.broadcasted_iota(jnp.int32, c.shape, 1) // C_HEADS == F_LANE0 // C_HEADS
    packed = None
    for p, piece in enumerate(_bf16_pieces(c * LOG2E)):
        piece = jnp.where(is_gate, piece, 0.0)
        piece = piece if p == 0 else pltpu.roll(piece, p * C_HEADS, 1)
        packed = piece if packed is None else packed + piece
    packed = packed.astype(BF16)

    for j in range(1, w_ref.shape[1] // tn):
        acc = tile(j)
        if j == cq_tile:
            acc = acc + jnp.dot(packed, pq_ref[...], preferred_element_type=F32)
        if j == cq_tile + 1:
            acc = acc - jnp.dot(packed, pk_ref[...], preferred_element_type=F32)
        o_ref[:, j * tn:(j + 1) * tn] = acc.astype(BF16)


def _placement_constants():
    w = C_HEADS * LANE
    pq = np.zeros((LANE, w), np.float32)
    pk = np.zeros((LANE, w), np.float32)
    for h in range(C_HEADS):
        for p in range(C_PARTS):
            pq[F_LANE0 + p * C_HEADS + h, h * LANE + HEAD_DIM + p] = 1.0
            pk[F_LANE0 + p * C_HEADS + h, h * LANE + HEAD_DIM + C_PARTS + p] = 1.0
    return jnp.asarray(pq, BF16), jnp.asarray(pk, BF16)


def _in_proj(x2, g, w_qk, brow, bf_row, wva, wvb, wvc, onea, onebc, *, b, tm, cq_tile):
    n, d = x2.shape
    s = n // b
    nt = s // tm
    ncol = w_qk.shape[1]
    ra, rbc = wva.shape[0], wvb.shape[0]
    tri = jnp.asarray(np.tril(np.ones((tm, tm), np.float32)), BF16)
    pq, pk = _placement_constants()
    const = lambda i: (0, 0)
    vt_map = lambda i: (i // nt, i % nt)
    return pl.pallas_call(
        functools.partial(_in_proj_kernel, row_tiles_per_seq=nt, cq_tile=cq_tile),
        out_shape=(jax.ShapeDtypeStruct((n, ncol), BF16),
                   jax.ShapeDtypeStruct((b * ra, s), BF16), jax.ShapeDtypeStruct((b * rbc, s), BF16),
                   jax.ShapeDtypeStruct((b * rbc, s), BF16)),
        grid=(n // tm,),
        in_specs=[
            pl.BlockSpec((tm, d), lambda i: (i, 0)),
            pl.BlockSpec((1, d), const),
            _resident((d, ncol), const),
            pl.BlockSpec((1, ncol), const),
            pl.BlockSpec((1, LANE), const),
            _resident((tm, tm), const),
            _resident(pq.shape, const),
            _resident(pk.shape, const),
            _resident((ra, d), const),
            _resident((rbc, d), const),
            _resident((rbc, d), const),
            pl.BlockSpec((ra, 1), const),
            pl.BlockSpec((rbc, 1), const),
        ],
        out_specs=(
            pl.BlockSpec((tm, ncol), lambda i: (i, 0)),
            pl.BlockSpec((ra, tm), vt_map),
            pl.BlockSpec((rbc, tm), vt_map),
            pl.BlockSpec((rbc, tm), vt_map),
        ),
        scratch_shapes=[pltpu.VMEM((1, LANE), F32)],
        compiler_params=_cparams("arbitrary"),
        name="in_proj",
    )(x2, g, w_qk, brow, bf_row, tri, pq, pk, wva, wvb, wvc, onea, onebc)


def _bias_tile_kernel(x_ref, o_ref, *, distance, band):
    d = distance(*(pl.program_id(a) for a in range(len(o_ref.shape) - 2)))
    x = jnp.broadcast_to(x_ref[0, 0], (T, 2 * T))
    tile = pltpu.roll(x, T + 1, 1, stride=1, stride_axis=0)[:, :T] * LOG2E
    ahead = (lax.broadcasted_iota(jnp.int32, (T, T), 0) // CHUNK
             - lax.broadcasted_iota(jnp.int32, (T, T), 1) // CHUNK
             - d * (T // CHUNK))
    if band:
        tile = jnp.where(ahead >= -B_LEFT_CHUNKS, tile, NEG_INF)
    o_ref[...] = jnp.where(ahead <= 0, tile, NEG_INF).reshape(o_ref.shape)


def _bias_tiles(x, lead, distance, d_min, *, band, name):
    heads = x.shape[1]
    zeros = (0,) * 2
    return pl.pallas_call(
        functools.partial(_bias_tile_kernel, distance=distance, band=band),
        out_shape=jax.ShapeDtypeStruct((heads, *lead, T, T), F32),
        grid=(heads, *lead),
        in_specs=[pl.BlockSpec((1, 1, 1, 2 * T),
                               lambda h, *idx: (distance(h, *idx) - d_min, h, 0, 0))],
        out_specs=pl.BlockSpec((1,) * (1 + len(lead)) + (T, T), lambda h, *idx: (h, *idx, *zeros)),
        compiler_params=_cparams(*(("parallel",) * (1 + len(lead)))),
        name=name,
    )(x)


def _t5_distance(h, parity, which, half):
    return parity + 2 * (1 - which) - half


def _band_distance(h, first, half):
    return 1 - half - first


def _softmax_step(s, vt, m_ref, acc_ref, col_max=None):
    rows = vt.shape[0]
    m_prev = m_ref[...]
    if col_max is None:
        col_max = jnp.max(s, axis=0, keepdims=True)
    m_new = jnp.maximum(m_prev, col_max)
    alpha = jnp.exp2(m_prev - m_new)
    p = jnp.exp2(s - m_new).astype(BF16)
    acc_ref[:rows, :] = acc_ref[:rows, :] * alpha + jnp.dot(vt, p, preferred_element_type=F32)
    m_ref[...] = m_new


def _keys(j, n_blocks=1):
    return pl.ds(pl.multiple_of(j * T, T), n_blocks * T)


def _step_keys(t):
    return _keys(2 * t, 2)


def _init(m_ref, acc_ref):
    m_ref[...] = jnp.full_like(m_ref, NEG_INF)
    acc_ref[...] = jnp.zeros_like(acc_ref)


def _attn_a_kernel(q_ref, k_ref, vt_ref, tile_ref, lp_ref, sg_ref, o_ref,
                   m0_ref, m1_ref, acc0_ref, acc1_ref, s0a_ref, s0b_ref, s1a_ref, s1b_ref,
                   smax_ref, *, lam_init):
    i = pl.program_id(2)
    streams = ((m0_ref, acc0_ref), (m1_ref, acc1_ref))
    buffers = ((s0a_ref, s0b_ref), (s1a_ref, s1b_ref))
    for m_ref, acc_ref in streams:
        _init(m_ref, acc_ref)
    q = q_ref[...]
    parity = i % 2
    n_steps = i // 2 + 1
    n_far = jnp.maximum(n_steps - 2, 0)

    def fill(which, t):
        for sub in range(2):
            lanes = slice(sub * LANE, (sub + 1) * LANE)
            s = _dot_nt(k_ref[_step_keys(t), lanes], q[:, lanes])
            buffers[sub][which][...] = s
            smax_ref[2 * sub + which] = jnp.max(s, axis=0, keepdims=True)

    def consume(which, t, near=None):
        vt = vt_ref[:, _step_keys(t)]
        for sub, (m_ref, acc_ref) in enumerate(streams):
            s = buffers[sub][which][...]
            col_max = smax_ref[2 * sub + which]
            if near is not None:
                s = s + tile_ref[0, parity, near]
                col_max = None
            _softmax_step(s, vt, m_ref, acc_ref, col_max)

    fill(0, 0)

    def far_body(u, carry):
        fill(1, 2 * u + 1)
        consume(0, 2 * u)
        fill(0, 2 * u + 2)
        consume(1, 2 * u + 1)
        return carry

    lax.fori_loop(0, n_far // 2, far_body, 0)

    @pl.when(n_steps == 1)
    def _():
        consume(0, 0, near=1)

    @pl.when(jnp.logical_and(n_steps >= 2, n_far % 2 == 0))
    def _():
        fill(1, n_steps - 1)
        consume(0, n_steps - 2, near=0)
        consume(1, n_steps - 1, near=1)

    @pl.when(n_far % 2 == 1)
    def _():
        fill(1, n_far)
        consume(0, n_far - 1)
        fill(0, n_far + 1)
        consume(1, n_far, near=0)
        consume(0, n_far + 1, near=1)

    lp = lp_ref[...]
    lam = (jnp.exp(jnp.sum(lp[0:1] * lp[1:2], axis=-1, keepdims=True))
           - jnp.exp(jnp.sum(lp[2:3] * lp[3:4], axis=-1, keepdims=True)) + lam_init)
    acc0 = acc0_ref[...]
    acc1 = acc1_ref[...]
    ot = (acc0[:A_VDIM] / acc0[A_VDIM:A_VDIM + 1]
          - lam * (acc1[:A_VDIM] / acc1[A_VDIM:A_VDIM + 1]))
    ot = _rms(ot, sg_ref[...], axis=0) * (1.0 - lam_init)
    o_ref[...] = ot.T.astype(BF16)


def _attn_a(qk, vt, tiles, lp, sg, *, b, s, lam_init):
    nq = s // T
    wa = 2 * LANE
    return pl.pallas_call(
        functools.partial(_attn_a_kernel, lam_init=lam_init),
        out_shape=jax.ShapeDtypeStruct((b * s, A_HEADS * A_VDIM), BF16),
        grid=(b, A_HEADS, nq),
        in_specs=[
            pl.BlockSpec((T, wa), lambda bi, h, i: (bi * nq + i, h)),
            _resident((s, wa), lambda bi, h, i: (bi, A_HEADS + h)),
            _resident((VA_ROWS, s), lambda bi, h, i: (bi * A_HEADS + h, 0)),
            _resident((1, 2, 2, 2 * T, T), lambda bi, h, i: (h, 0, 0, 0, 0)),
            pl.BlockSpec((4, HEAD_DIM), lambda bi, h, i: (0, 0)),
            pl.BlockSpec((A_VDIM, 1), lambda bi, h, i: (0, 0)),
        ],
        out_specs=pl.BlockSpec((T, A_VDIM), lambda bi, h, i: (bi * nq + i, h)),
        scratch_shapes=[pltpu.VMEM((1, T), F32), pltpu.VMEM((1, T), F32),
                        pltpu.VMEM((VA_ROWS, T), F32), pltpu.VMEM((VA_ROWS, T), F32)]
        + [pltpu.VMEM((2 * T, T), F32)] * 4 + [pltpu.VMEM((4, 1, T), F32)],
        compiler_params=_cparams("parallel", "parallel", "arbitrary"),
        name="attn_a",
    )(qk, qk, vt, tiles, lp, sg)


def _pair_output(acc_refs):
    halves = []
    for acc_ref in acc_refs:
        acc = acc_ref[...]
        halves.append(acc[:HEAD_DIM] / acc[HEAD_DIM:HEAD_DIM + 1])
    return jnp.concatenate(halves, axis=0).T.astype(BF16)


def _pair_values(vt_ref, sub, keys):
    return vt_ref[sub * VBC_ROWS:(sub + 1) * VBC_ROWS, keys]


B_QBLOCKS = 4


def _attn_b_kernel(q_ref, k_ref, vt_ref, tile_ref, o_ref, s0a_ref, s0b_ref, s1a_ref, s1b_ref,
                   smax_ref):
    buffers = ((s0a_ref, s0b_ref), (s1a_ref, s1b_ref))
    lane_head = lax.broadcasted_iota(jnp.int32, (T, LANE), 1) // HEAD_DIM

    def keys_of(r):
        g = pl.program_id(2) * B_QBLOCKS + r
        return g, _keys(jnp.maximum(g - 1, 0), 2)

    def fill(which, r):
        g, keys = keys_of(r)
        first = (g == 0).astype(jnp.int32)
        k = k_ref[keys, :]
        q = q_ref[r * T:(r + 1) * T, :]
        for sub in range(2):
            q_head = jnp.where(lane_head == sub, q, jnp.zeros_like(q))
            s = _dot_nt(k, q_head) + tile_ref[sub, first]
            buffers[sub][which][...] = s
            smax_ref[2 * sub + which] = jnp.max(s, axis=0, keepdims=True)

    def consume(which, r):
        _, keys = keys_of(r)
        halves = []
        for sub in range(2):
            p = jnp.exp2(buffers[sub][which][...] - smax_ref[2 * sub + which]).astype(BF16)
            acc = jnp.dot(_pair_values(vt_ref, sub, keys), p, preferred_element_type=F32)
            halves.append(acc[:HEAD_DIM] / acc[HEAD_DIM:HEAD_DIM + 1])
        o_ref[r * T:(r + 1) * T, :] = jnp.concatenate(halves, axis=0).T.astype(BF16)

    fill(0, 0)
    for r in range(B_QBLOCKS):
        if r + 1 < B_QBLOCKS:
            fill((r + 1) % 2, r + 1)
        consume(r % 2, r)


def _attn_b(qk, vt, tiles, *, b, s, q_block0, k_block0):
    groups = s // (B_QBLOCKS * T)
    pairs = B_HEADS // 2
    return pl.pallas_call(
        _attn_b_kernel,
        out_shape=jax.ShapeDtypeStruct((b * s, B_HEADS * HEAD_DIM), BF16),
        grid=(b, pairs, groups),
        in_specs=[
            pl.BlockSpec((B_QBLOCKS * T, LANE), lambda bi, hp, i: (bi * groups + i, q_block0 + hp)),
            _resident((s, LANE), lambda bi, hp, i: (bi, k_block0 + hp)),
            _resident((2 * VBC_ROWS, s), lambda bi, hp, i: (bi * pairs + hp, 0)),
            _resident((2, 2, 2 * T, T), lambda bi, hp, i: (hp, 0, 0, 0)),
        ],
        out_specs=pl.BlockSpec((B_QBLOCKS * T, 2 * HEAD_DIM),
                               lambda bi, hp, i: (bi * groups + i, hp)),
        scratch_shapes=[pltpu.VMEM((2 * T, T), F32)] * 4 + [pltpu.VMEM((4, 1, T), F32)],
        compiler_params=_cparams("parallel", "parallel", "arbitrary"),
        name="attn_b",
    )(qk, qk, vt, tiles)


def _attn_c_kernel(q_ref, k_ref, vt_ref, o_ref, m0_ref, m1_ref, acc0_ref, acc1_ref,
                   s0a_ref, s0b_ref, s1a_ref, s1b_ref, smax_ref):
    i = pl.program_id(2)
    streams = ((m0_ref, acc0_ref), (m1_ref, acc1_ref))
    buffers = ((s0a_ref, s0b_ref), (s1a_ref, s1b_ref))
    for m_ref, acc_ref in streams:
        _init(m_ref, acc_ref)
    q = q_ref[...]
    n_plain = i // 2

    def fill(which, t):
        for sub in range(2):
            lanes = slice(sub * LANE, (sub + 1) * LANE)
            s = _dot_nt(k_ref[_step_keys(t), lanes], q[:, lanes])
            buffers[sub][which][...] = s
            smax_ref[2 * sub + which] = jnp.max(s, axis=0, keepdims=True)

    def consume(which, t, masked=False):
        if masked:
            ahead = (lax.broadcasted_iota(jnp.int32, (2 * T, T), 0)
                     - lax.broadcasted_iota(jnp.int32, (2 * T, T), 1))
            causal = ahead <= (i - 2 * t) * T
        for sub, (m_ref, acc_ref) in enumerate(streams):
            s = buffers[sub][which][...]
            col_max = smax_ref[2 * sub + which]
            if masked:
                s = jnp.where(causal, s, NEG_INF)
                col_max = None
            _softmax_step(s, _pair_values(vt_ref, sub, _step_keys(t)), m_ref, acc_ref, col_max)

    fill(0, 0)

    def body(u, carry):
        fill(1, 2 * u + 1)
        consume(0, 2 * u)
        fill(0, 2 * u + 2)
        consume(1, 2 * u + 1)
        return carry

    lax.fori_loop(0, n_plain // 2, body, 0)

    @pl.when(n_plain % 2 == 0)
    def _():
        consume(0, n_plain, masked=True)

    @pl.when(n_plain % 2 == 1)
    def _():
        fill(1, n_plain)
        consume(0, n_plain - 1)
        consume(1, n_plain, masked=True)

    o_ref[...] = _pair_output((acc0_ref, acc1_ref))


def _attn_bc(kernel, q_arr, k_arr, vt, extra, extra_specs, *, b, s, heads, width, q_block0,
             k_block0, name, score_buffers=0):
    nq = s // T
    pairs = heads // 2
    return pl.pallas_call(
        kernel,
        out_shape=jax.ShapeDtypeStruct((b * s, heads * HEAD_DIM), BF16),
        grid=(b, pairs, nq),
        in_specs=[
            pl.BlockSpec((T, width), lambda bi, hp, i: (bi * nq + i, q_block0 + hp)),
            _resident((s, width), lambda bi, hp, i: (bi, k_block0 + hp)),
            _resident((2 * VBC_ROWS, s), lambda bi, hp, i: (bi * pairs + hp, 0)),
            *extra_specs,
        ],
        out_specs=pl.BlockSpec((T, 2 * HEAD_DIM), lambda bi, hp, i: (bi * nq + i, hp)),
        scratch_shapes=[pltpu.VMEM((1, T), F32)] * 2 + [pltpu.VMEM((VBC_ROWS, T), F32)] * 2
        + [pltpu.VMEM((2 * T, T), F32)] * score_buffers
        + [pltpu.VMEM((score_buffers, 1, T), F32)] * (score_buffers > 0),
        compiler_params=_cparams("parallel", "parallel", "arbitrary"),
        name=name,
    )(q_arr, k_arr, vt, *extra)


def _merge_kernel(x_ref, g_ref, wg_ref, oa_ref, ob_ref, oc_ref, wa_ref, wb_ref, wc_ref,
                  wo_ref, o_ref):
    x = x_ref[...]
    d = x.shape[1]
    h = _rms(x, g_ref[...]).astype(BF16)
    gates = jax.nn.sigmoid(jnp.dot(h, wg_ref[...], preferred_element_type=F32))
    merged = None
    for n, (o_br, w_br) in enumerate(((oa_ref, wa_ref), (ob_ref, wb_ref), (oc_ref, wc_ref))):
        term = gates[:, n * d:(n + 1) * d] * jnp.dot(o_br[...], w_br[...],
                                                     preferred_element_type=F32)
        merged = term if merged is None else merged + term
    o_ref[...] = x + jnp.dot(merged.astype(BF16), wo_ref[...], preferred_element_type=F32)


def _merge(x2, g, wg, oa, ob, oc, wa, wb, wc, wo, *, tm):
    n, d = x2.shape
    row = lambda i: (i, 0)
    const = lambda i: (0, 0)
    return pl.pallas_call(
        _merge_kernel,
        out_shape=jax.ShapeDtypeStruct((n, d), F32),
        grid=(n // tm,),
        in_specs=[
            pl.BlockSpec((tm, d), row),
            pl.BlockSpec((1, d), const),
            _resident(wg.shape, const),
            pl.BlockSpec((tm, oa.shape[1]), row),
            pl.BlockSpec((tm, ob.shape[1]), row),
            pl.BlockSpec((tm, oc.shape[1]), row),
            _resident(wa.shape, const),
            _resident(wb.shape, const),
            _resident(wc.shape, const),
            _resident(wo.shape, const),
        ],
        out_specs=pl.BlockSpec((tm, d), row),
        compiler_params=_cparams("parallel"),
        name="merge",
    )(x2, g, wg, oa, ob, oc, wa, wb, wc, wo)


def _ffn_kernel(x_ref, g_ref, wg_ref, wu_ref, wd_ref, gf_ref, o_ref, *, n_chunks, final_norm):
    x = x_ref[...]
    h = _rms(x, g_ref[...]).astype(BF16)
    fc = wg_ref.shape[1] // n_chunks
    y = x
    for c in range(n_chunks):
        cols = slice(c * fc, (c + 1) * fc)
        gt = jnp.dot(h, wg_ref[:, cols], preferred_element_type=F32)
        up = jnp.dot(h, wu_ref[:, cols], preferred_element_type=F32)
        act = (gt * jax.nn.sigmoid(gt) * up).astype(BF16)
        y = y + jnp.dot(act, wd_ref[cols, :], preferred_element_type=F32)
    if final_norm:
        y = _rms(y, gf_ref[...])
    o_ref[...] = y


def _ffn(x2, g, wg, wu, wd, gf, *, tm, n_chunks, final_norm):
    n, d = x2.shape
    row = lambda i: (i, 0)
    const = lambda i: (0, 0)
    return pl.pallas_call(
        functools.partial(_ffn_kernel, n_chunks=n_chunks, final_norm=final_norm),
        out_shape=jax.ShapeDtypeStruct((n, d), F32),
        grid=(n // tm,),
        in_specs=[
            pl.BlockSpec((tm, d), row),
            pl.BlockSpec((1, d), const),
            _resident(wg.shape, const),
            _resident(wu.shape, const),
            _resident(wd.shape, const),
            pl.BlockSpec((1, d), const),
        ],
        out_specs=pl.BlockSpec((tm, d), row),
        compiler_params=_cparams("parallel"),
        name="ffn",
    )(x2, g, wg, wu, wd, gf)


def _pad_heads(w, heads, width, slot):
    d = w.shape[0]
    w = w.reshape(d, heads, width)
    return jnp.pad(w, ((0, 0), (0, 0), (0, slot - width))).reshape(d, heads * slot)


def _ones_col(heads, slot, at):
    col = np.zeros((heads, slot), np.float32)
    col[:, at] = 1.0
    return jnp.asarray(col.reshape(heads * slot, 1))


def _t5_bucket(rel):
    nb = T5_BUCKETS // 2
    max_exact = nb // 2
    n = jnp.abs(rel)
    nf = jnp.maximum(n, 1).astype(jnp.float32)
    large = max_exact + (jnp.log(nf / max_exact) / math.log(T5_MAX_DIST / max_exact)
                         * (nb - max_exact)).astype(jnp.int32)
    large = jnp.minimum(large, nb - 1)
    return jnp.where(rel > 0, nb, 0) + jnp.where(n < max_exact, n, large)


def _tile_rel(d_min, d_max):
    return ((T - 1 - jnp.arange(2 * T))[None, :]
            - jnp.arange(d_min, d_max + 1)[:, None] * T)


def _rel_vectors(table, idx):
    return jnp.moveaxis(table[idx], -1, 1).astype(F32)[:, :, None, :]


def kernel(x, norm_mix_g, w_in, b_forget, diff_lambda, diff_subln_g, t5_table, b_rel_table,
           w_br_a, w_br_b, w_br_c, w_out, norm_ffn_g, w_gate_up, w_down, final_norm_g):
    b, s, d = x.shape
    depth = w_in.shape[0]
    assert s % (B_QBLOCKS * T) == 0 and B_QBLOCKS % 2 == 0 and T // CHUNK == B_LEFT_CHUNKS
    n = b * s
    scale = HEAD_DIM ** -0.5 * LOG2E
    a_qk_w = A_HEADS * 2 * HEAD_DIM
    a_v_w = A_HEADS * A_VDIM
    bw = B_HEADS * HEAD_DIM
    cw = C_HEADS * HEAD_DIM
    d_ff = w_down.shape[1]
    x2 = x.reshape(n, d)

    t5_far = t5_table[T5_BUCKETS // 2 - 1].astype(F32)
    t5_tiles = _bias_tiles(
        _rel_vectors(t5_table, _t5_bucket(_tile_rel(-1, A_NEAR))) - t5_far[None, :, None, None],
        (2, 2, 2), _t5_distance, -1, band=False, name="bias_tiles_t5",
    ).reshape(A_HEADS, 2, 2, 2 * T, T)
    far_pieces, rest = [], t5_far * LOG2E
    for _ in range(C_PARTS):
        piece = rest.astype(BF16).astype(F32)
        far_pieces.append(piece)
        rest = rest - piece
    far_pieces = jnp.repeat(jnp.stack(far_pieces, axis=-1), 2, axis=0)
    spare = slice(HEAD_DIM, HEAD_DIM + C_PARTS)
    brow_qa = jnp.zeros((2 * A_HEADS, LANE), F32).at[:, spare].set(1.0)
    brow_ka = jnp.zeros((2 * A_HEADS, LANE), F32).at[:, spare].set(far_pieces)
    brow_qc = jnp.zeros((C_HEADS, LANE), F32).at[:, HEAD_DIM + C_PARTS:HEAD_DIM + 2 * C_PARTS].set(1.0)
    brow_kc = jnp.zeros((C_HEADS, LANE), F32).at[:, spare].set(1.0)
    b_slot0 = 4 * A_HEADS
    c_slot0 = b_slot0 + B_HEADS
    brow = jnp.concatenate([brow_qa.reshape(1, -1), brow_ka.reshape(1, -1),
                            jnp.zeros((1, 2 * B_HEADS * HEAD_DIM), F32),
                            brow_qc.reshape(1, -1), brow_kc.reshape(1, -1)], axis=1)
    band_idx = jnp.clip(_tile_rel(-1, 1), -B_REL_CLIP, CHUNK - 1) + B_REL_CLIP
    onea = _ones_col(A_HEADS, VA_ROWS, A_VDIM)
    onebc = _ones_col(B_HEADS, VBC_ROWS, HEAD_DIM)

    for l in range(depth):
        w = w_in[l]
        pieces, at = [], 0
        for width in (a_qk_w, a_qk_w, a_v_w, bw, bw, bw, cw, cw, cw, C_HEADS, N_BRANCH * d):
            pieces.append(w[:, at:at + width])
            at += width
        a_q, a_k, a_v, b_q, b_k, b_v, c_q, c_k, c_v, c_f, w_gates = pieces
        w_qk = jnp.concatenate([
            _pad_heads(a_q * scale, 2 * A_HEADS, HEAD_DIM, LANE),
            _pad_heads(a_k, 2 * A_HEADS, HEAD_DIM, LANE),
            b_q * scale,
            b_k,
            _pad_heads(c_q * scale, C_HEADS, HEAD_DIM, LANE),
            _pad_heads(c_k, C_HEADS, HEAD_DIM, LANE),
        ], axis=1).at[:, F_LANE0:F_LANE0 + C_HEADS].set(c_f).astype(BF16)
        wva = _pad_heads(a_v, A_HEADS, A_VDIM, VA_ROWS).T.astype(BF16)
        wvb = _pad_heads(b_v, B_HEADS, HEAD_DIM, VBC_ROWS).T.astype(BF16)
        wvc = _pad_heads(c_v, C_HEADS, HEAD_DIM, VBC_ROWS).T.astype(BF16)

        bf_row = jnp.zeros((1, LANE), F32).at[0, F_LANE0:F_LANE0 + C_HEADS].set(b_forget[l])
        qk, vta, vtb, vtc = _in_proj(
            x2, norm_mix_g[l][None], w_qk, brow, bf_row, wva, wvb, wvc, onea, onebc,
            b=b, tm=min(512, s), cq_tile=c_slot0 // C_HEADS)

        lam_init = 0.8 - 0.6 * math.exp(-0.3 * l)
        o_a = _attn_a(qk, vta, t5_tiles, diff_lambda[l].astype(F32),
                      diff_subln_g[l][:, None].astype(F32), b=b, s=s, lam_init=lam_init)

        band_tiles = _bias_tiles(
            _rel_vectors(b_rel_table[l], band_idx), (2, 2), _band_distance, -1, band=True,
            name="bias_tiles_band").reshape(B_HEADS, 2, 2 * T, T)
        o_b = _attn_b(qk, vtb, band_tiles, b=b, s=s, q_block0=b_slot0,
                      k_block0=b_slot0 + B_HEADS // 2)

        o_c = _attn_bc(_attn_c_kernel, qk, qk, vtc, (), (), b=b, s=s, heads=C_HEADS,
                       width=2 * LANE, q_block0=c_slot0 // 2, k_block0=(c_slot0 + C_HEADS) // 2,
                       name="attn_c", score_buffers=4)

        x2 = _merge(
            x2, norm_mix_g[l][None], w_gates.astype(BF16), o_a, o_b, o_c,
            w_br_a[l].astype(BF16), w_br_b[l].astype(BF16), w_br_c[l].astype(BF16),
            w_out[l].astype(BF16), tm=min(512, n))

        x2 = _ffn(
            x2, norm_ffn_g[l][None], w_gate_up[l][:, :d_ff].astype(BF16),
            w_gate_up[l][:, d_ff:].astype(BF16), w_down[l].astype(BF16),
            final_norm_g[None], tm=min(512, n), n_chunks=2, final_norm=(l == depth - 1))

    return x2.reshape(b, s, d)
```

```python
import functools
import math

import numpy as np
import jax
import jax.numpy as jnp
from jax import lax
from jax.experimental import pallas as pl
from jax.experimental.pallas import tpu as pltpu

CHUNK = 64
HEAD_DIM = 64
A_HEADS = 4
A_VDIM = 2 * HEAD_DIM
B_HEADS = 8
B_LEFT_CHUNKS = 8
B_REL_CLIP = 128
C_HEADS = 8
T5_BUCKETS = 32
T5_MAX_DIST = 2048
N_BRANCH = 3
RMS_EPS = 1e-6
NEG_INF = -1e30
LOG2E = math.log2(math.e)

LANE = 128
BF16_ROWS = 16
T = 512
A_NEAR = 3
VA_ROWS = A_VDIM + BF16_ROWS
VBC_ROWS = HEAD_DIM + BF16_ROWS
C_PARTS = 3
VMEM_LIMIT = 56 * 1024 * 1024

F32 = jnp.float32
BF16 = jnp.bfloat16


def _cparams(*sem):
    return pltpu.CompilerParams(dimension_semantics=sem, vmem_limit_bytes=VMEM_LIMIT)


def _resident(block_shape, index_map):
    return pl.BlockSpec(block_shape, index_map, pipeline_mode=pl.Buffered(1))


def _rms(x, g, axis=-1):
    y = x * lax.rsqrt(jnp.mean(x * x, axis=axis, keepdims=True) + RMS_EPS)
    return y * g


def _dot_nt(a, b):
    return lax.dot_general(a, b, (((1,), (1,)), ((), ())), preferred_element_type=F32)


F_LANE0 = HEAD_DIM + C_HEADS


def _bf16_pieces(x):
    pieces, rest = [], x
    for _ in range(C_PARTS):
        piece = rest.astype(BF16).astype(F32)
        pieces.append(piece)
        rest = rest - piece
    return pieces


def _in_proj_kernel(x_ref, g_ref, w_ref, brow_ref, bf_ref, tri_ref, pq_ref, pk_ref,
                    wva_ref, wvb_ref, wvc_ref, onea_ref, onebc_ref,
                    o_ref, va_ref, vb_ref, vc_ref, carry_ref, *, row_tiles_per_seq, cq_tile):
    h = _rms(x_ref[...], g_ref[...]).astype(BF16)
    va_ref[...] = (_dot_nt(wva_ref[...], h) + onea_ref[...]).astype(BF16)
    vb_ref[...] = (_dot_nt(wvb_ref[...], h) + onebc_ref[...]).astype(BF16)
    vc_ref[...] = (_dot_nt(wvc_ref[...], h) + onebc_ref[...]).astype(BF16)
    tn = pq_ref.shape[1]

    def tile(j):
        cols = slice(j * tn, (j + 1) * tn)
        return jnp.dot(h, w_ref[:, cols], preferred_element_type=F32) + brow_ref[:, cols]

    first = tile(0)
    o_ref[:, :tn] = first.astype(BF16)

    @pl.when(pl.program_id(0) % row_tiles_per_seq == 0)
    def _():
        carry_ref[...] = jnp.zeros_like(carry_ref)

    x = first[:, :LANE] + bf_ref[...]
    log_f = -(jnp.maximum(-x, 0.0) + jnp.log1p(jnp.exp(-jnp.abs(x))))
    pieces = jnp.concatenate([p.astype(BF16) for p in _bf16_pieces(log_f)], axis=1)
    sums = jnp.dot(tri_ref[...], pieces, preferred_element_type=F32)
    c = carry_ref[...]
    for p in range(C_PARTS):
        c = c + sums[:, p * LANE:(p + 1) * LANE]
    carry_ref[...] = c[c.shape[0] - 1:, :]

    is_gate = lax.broadcasted_iota(jnp.int32, c.shape, 1) // C_HEADS == F_LANE0 // C_HEADS
    packed = None
    for p, piece in enumerate(_bf16_pieces(c * LOG2E)):
        piece = jnp.where(is_gate, piece, 0.0)
        piece = piece if p == 0 else pltpu.roll(piece, p * C_HEADS, 1)
        packed = piece if packed is None else packed + piece
    packed = packed.astype(BF16)

    for j in range(1, w_ref.shape[1] // tn):
        acc = tile(j)
        if j == cq_tile:
            acc = acc + jnp.dot(packed, pq_ref[...], preferred_element_type=F32)
        if j == cq_tile + 1:
            acc = acc - jnp.dot(packed, pk_ref[...], preferred_element_type=F32)
        o_ref[:, j * tn:(j + 1) * tn] = acc.astype(BF16)


def _placement_constants():
    w = C_HEADS * LANE
    pq = np.zeros((LANE, w), np.float32)
    pk = np.zeros((LANE, w), np.float32)
    for h in range(C_HEADS):
        for p in range(C_PARTS):
            pq[F_LANE0 + p * C_HEADS + h, h * LANE + HEAD_DIM + p] = 1.0
            pk[F_LANE0 + p * C_HEADS + h, h * LANE + HEAD_DIM + C_PARTS + p] = 1.0
    return jnp.asarray(pq, BF16), jnp.asarray(pk, BF16)


def _in_proj(x2, g, w_qk, brow, bf_row, wva, wvb, wvc, onea, onebc, *, b, tm, cq_tile):
    n, d = x2.shape
    s = n // b
    nt = s // tm
    ncol = w_qk.shape[1]
    ra, rbc = wva.shape[0], wvb.shape[0]
    tri = jnp.asarray(np.tril(np.ones((tm, tm), np.float32)), BF16)
    pq, pk = _placement_constants()
    const = lambda i: (0, 0)
    vt_map = lambda i: (i // nt, i % nt)
    return pl.pallas_call(
        functools.partial(_in_proj_kernel, row_tiles_per_seq=nt, cq_tile=cq_tile),
        out_shape=(jax.ShapeDtypeStruct((n, ncol), BF16),
                   jax.ShapeDtypeStruct((b * ra, s), BF16), jax.ShapeDtypeStruct((b * rbc, s), BF16),
                   jax.ShapeDtypeStruct((b * rbc, s), BF16)),
        grid=(n // tm,),
        in_specs=[
            pl.BlockSpec((tm, d), lambda i: (i, 0)),
            pl.BlockSpec((1, d), const),
            _resident((d, ncol), const),
            pl.BlockSpec((1, ncol), const),
            pl.BlockSpec((1, LANE), const),
            _resident((tm, tm), const),
            _resident(pq.shape, const),
            _resident(pk.shape, const),
            _resident((ra, d), const),
            _resident((rbc, d), const),
            _resident((rbc, d), const),
            pl.BlockSpec((ra, 1), const),
            pl.BlockSpec((rbc, 1), const),
        ],
        out_specs=(
            pl.BlockSpec((tm, ncol), lambda i: (i, 0)),
            pl.BlockSpec((ra, tm), vt_map),
            pl.BlockSpec((rbc, tm), vt_map),
            pl.BlockSpec((rbc, tm), vt_map),
        ),
        scratch_shapes=[pltpu.VMEM((1, LANE), F32)],
        compiler_params=_cparams("arbitrary"),
        name="in_proj",
    )(x2, g, w_qk, brow, bf_row, tri, pq, pk, wva, wvb, wvc, onea, onebc)


def _bias_tile_kernel(x_ref, o_ref, *, distance, band):
    d = distance(*(pl.program_id(a) for a in range(len(o_ref.shape) - 2)))
    x = jnp.broadcast_to(x_ref[0, 0], (T, 2 * T))
    tile = pltpu.roll(x, T + 1, 1, stride=1, stride_axis=0)[:, :T] * LOG2E
    ahead = (lax.broadcasted_iota(jnp.int32, (T, T), 0) // CHUNK
             - lax.broadcasted_iota(jnp.int32, (T, T), 1) // CHUNK
             - d * (T // CHUNK))
    if band:
        tile = jnp.where(ahead >= -B_LEFT_CHUNKS, tile, NEG_INF)
    o_ref[...] = jnp.where(ahead <= 0, tile, NEG_INF).reshape(o_ref.shape)


def _bias_tiles(x, lead, distance, d_min, *, band, name):
    heads = x.shape[1]
    zeros = (0,) * 2
    return pl.pallas_call(
        functools.partial(_bias_tile_kernel, distance=distance, band=band),
        out_shape=jax.ShapeDtypeStruct((heads, *lead, T, T), F32),
        grid=(heads, *lead),
        in_specs=[pl.BlockSpec((1, 1, 1, 2 * T),
                               lambda h, *idx: (distance(h, *idx) - d_min, h, 0, 0))],
        out_specs=pl.BlockSpec((1,) * (1 + len(lead)) + (T, T), lambda h, *idx: (h, *idx, *zeros)),
        compiler_params=_cparams(*(("parallel",) * (1 + len(lead)))),
        name=name,
    )(x)


def _t5_distance(h, parity, which, half):
    return parity + 2 * (1 - which) - half


def _band_distance(h, first, half):
    return 1 - half - first


def _softmax_step(s, vt, m_ref, acc_ref, col_max=None):
    rows = vt.shape[0]
    m_prev = m_ref[...]
    if col_max is None:
        col_max = jnp.max(s, axis=0, keepdims=True)
    m_new = jnp.maximum(m_prev, col_max)
    alpha = jnp.exp2(m_prev - m_new)
    p = jnp.exp2(s - m_new).astype(BF16)
    acc_ref[:rows, :] = acc_ref[:rows, :] * alpha + jnp.dot(vt, p, preferred_element_type=F32)
    m_ref[...] = m_new


def _keys(j, n_blocks=1):
    return pl.ds(pl.multiple_of(j * T, T), n_blocks * T)


def _step_keys(t):
    return _keys(2 * t, 2)


def _init(m_ref, acc_ref):
    m_ref[...] = jnp.full_like(m_ref, NEG_INF)
    acc_ref[...] = jnp.zeros_like(acc_ref)


def _attn_a_kernel(q_ref, k_ref, vt_ref, tile_ref, lp_ref, sg_ref, o_ref,
                   m0_ref, m1_ref, acc0_ref, acc1_ref, s0a_ref, s0b_ref, s1a_ref, s1b_ref,
                   smax_ref, *, lam_init):
    i = pl.program_id(2)
    streams = ((m0_ref, acc0_ref), (m1_ref, acc1_ref))
    buffers = ((s0a_ref, s0b_ref), (s1a_ref, s1b_ref))
    for m_ref, acc_ref in streams:
        _init(m_ref, acc_ref)
    q = q_ref[...]
    parity = i % 2
    n_steps = i // 2 + 1
    n_far = jnp.maximum(n_steps - 2, 0)

    def fill(which, t):
        for sub in range(2):
            lanes = slice(sub * LANE, (sub + 1) * LANE)
            s = _dot_nt(k_ref[_step_keys(t), lanes], q[:, lanes])
            buffers[sub][which][...] = s
            smax_ref[2 * sub + which] = jnp.max(s, axis=0, keepdims=True)

    def consume(which, t, near=None):
        vt = vt_ref[:, _step_keys(t)]
        for sub, (m_ref, acc_ref) in enumerate(streams):
            s = buffers[sub][which][...]
            col_max = smax_ref[2 * sub + which]
            if near is not None:
                s = s + tile_ref[0, parity, near]
                col_max = None
            _softmax_step(s, vt, m_ref, acc_ref, col_max)

    fill(0, 0)

    def far_body(u, carry):
        fill(1, 2 * u + 1)
        consume(0, 2 * u)
        fill(0, 2 * u + 2)
        consume(1, 2 * u + 1)
        return carry

    lax.fori_loop(0, n_far // 2, far_body, 0)

    @pl.when(n_steps == 1)
    def _():
        consume(0, 0, near=1)

    @pl.when(jnp.logical_and(n_steps >= 2, n_far % 2 == 0))
    def _():
        fill(1, n_steps - 1)
        consume(0, n_steps - 2, near=0)
        consume(1, n_steps - 1, near=1)

    @pl.when(n_far % 2 == 1)
    def _():
        fill(1, n_far)
        consume(0, n_far - 1)
        fill(0, n_far + 1)
        consume(1, n_far, near=0)
        consume(0, n_far + 1, near=1)

    lp = lp_ref[...]
    lam = (jnp.exp(jnp.sum(lp[0:1] * lp[1:2], axis=-1, keepdims=True))
           - jnp.exp(jnp.sum(lp[2:3] * lp[3:4], axis=-1, keepdims=True)) + lam_init)
    acc0 = acc0_ref[...]
    acc1 = acc1_ref[...]
    ot = (acc0[:A_VDIM] / acc0[A_VDIM:A_VDIM + 1]
          - lam * (acc1[:A_VDIM] / acc1[A_VDIM:A_VDIM + 1]))
    ot = _rms(ot, sg_ref[...], axis=0) * (1.0 - lam_init)
    o_ref[...] = ot.T.astype(BF16)


def _attn_a(qk, vt, tiles, lp, sg, *, b, s, lam_init):
    nq = s // T
    wa = 2 * LANE
    return pl.pallas_call(
        functools.partial(_attn_a_kernel, lam_init=lam_init),
        out_shape=jax.ShapeDtypeStruct((b * s, A_HEADS * A_VDIM), BF16),
        grid=(b, A_HEADS, nq),
        in_specs=[
            pl.BlockSpec((T, wa), lambda bi, h, i: (bi * nq + i, h)),
            pl.BlockSpec((s, wa), lambda bi, h, i: (bi, A_HEADS + h)),
            pl.BlockSpec((VA_ROWS, s), lambda bi, h, i: (bi * A_HEADS + h, 0)),
            _resident((1, 2, 2, 2 * T, T), lambda bi, h, i: (h, 0, 0, 0, 0)),
            pl.BlockSpec((4, HEAD_DIM), lambda bi, h, i: (0, 0)),
            pl.BlockSpec((A_VDIM, 1), lambda bi, h, i: (0, 0)),
        ],
        out_specs=pl.BlockSpec((T, A_VDIM), lambda bi, h, i: (bi * nq + i, h)),
        scratch_shapes=[pltpu.VMEM((1, T), F32), pltpu.VMEM((1, T), F32),
                        pltpu.VMEM((VA_ROWS, T), F32), pltpu.VMEM((VA_ROWS, T), F32)]
        + [pltpu.VMEM((2 * T, T), F32)] * 4 + [pltpu.VMEM((4, 1, T), F32)],
        compiler_params=_cparams("parallel", "parallel", "arbitrary"),
        name="attn_a",
    )(qk, qk, vt, tiles, lp, sg)


def _pair_output(acc_refs):
    halves = []
    for acc_ref in acc_refs:
        acc = acc_ref[...]
        halves.append(acc[:HEAD_DIM] / acc[HEAD_DIM:HEAD_DIM + 1])
    return jnp.concatenate(halves, axis=0).T.astype(BF16)


def _pair_values(vt_ref, sub, keys):
    return vt_ref[sub * VBC_ROWS:(sub + 1) * VBC_ROWS, keys]


B_QBLOCKS = 4


def _attn_b_kernel(q_ref, k_ref, vt_ref, tile_ref, o_ref, s0a_ref, s0b_ref, s1a_ref, s1b_ref,
                   smax_ref):
    buffers = ((s0a_ref, s0b_ref), (s1a_ref, s1b_ref))
    lane_head = lax.broadcasted_iota(jnp.int32, (T, LANE), 1) // HEAD_DIM

    def keys_of(r):
        g = pl.program_id(2) * B_QBLOCKS + r
        return g, _keys(jnp.maximum(g - 1, 0), 2)

    def fill(which, r):
        g, keys = keys_of(r)
        first = (g == 0).astype(jnp.int32)
        k = k_ref[keys, :]
        q = q_ref[r * T:(r + 1) * T, :]
        for sub in range(2):
            q_head = jnp.where(lane_head == sub, q, jnp.zeros_like(q))
            s = _dot_nt(k, q_head) + tile_ref[sub, first]
            buffers[sub][which][...] = s
            smax_ref[2 * sub + which] = jnp.max(s, axis=0, keepdims=True)

    def consume(which, r):
        _, keys = keys_of(r)
        halves = []
        for sub in range(2):
            p = jnp.exp2(buffers[sub][which][...] - smax_ref[2 * sub + which]).astype(BF16)
            acc = jnp.dot(_pair_values(vt_ref, sub, keys), p, preferred_element_type=F32)
            halves.append(acc[:HEAD_DIM] / acc[HEAD_DIM:HEAD_DIM + 1])
        o_ref[r * T:(r + 1) * T, :] = jnp.concatenate(halves, axis=0).T.astype(BF16)

    fill(0, 0)
    for r in range(B_QBLOCKS):
        if r + 1 < B_QBLOCKS:
            fill((r + 1) % 2, r + 1)
        consume(r % 2, r)


def _attn_b(qk, vt, tiles, *, b, s, q_block0, k_block0):
    groups = s // (B_QBLOCKS * T)
    pairs = B_HEADS // 2
    return pl.pallas_call(
        _attn_b_kernel,
        out_shape=jax.ShapeDtypeStruct((b * s, B_HEADS * HEAD_DIM), BF16),
        grid=(b, pairs, groups),
        in_specs=[
            pl.BlockSpec((B_QBLOCKS * T, LANE), lambda bi, hp, i: (bi * groups + i, q_block0 + hp)),
            pl.BlockSpec((s, LANE), lambda bi, hp, i: (bi, k_block0 + hp)),
            pl.BlockSpec((2 * VBC_ROWS, s), lambda bi, hp, i: (bi * pairs + hp, 0)),
            pl.BlockSpec((2, 2, 2 * T, T), lambda bi, hp, i: (hp, 0, 0, 0)),
        ],
        out_specs=pl.BlockSpec((B_QBLOCKS * T, 2 * HEAD_DIM),
                               lambda bi, hp, i: (bi * groups + i, hp)),
        scratch_shapes=[pltpu.VMEM((2 * T, T), F32)] * 4 + [pltpu.VMEM((4, 1, T), F32)],
        compiler_params=_cparams("parallel", "parallel", "arbitrary"),
        name="attn_b",
    )(qk, qk, vt, tiles)


def _attn_c_kernel(q_ref, k_ref, vt_ref, o_ref, m0_ref, m1_ref, acc0_ref, acc1_ref,
                   s0a_ref, s0b_ref, s1a_ref, s1b_ref, smax_ref):
    i = pl.program_id(2)
    streams = ((m0_ref, acc0_ref), (m1_ref, acc1_ref))
    buffers = ((s0a_ref, s0b_ref), (s1a_ref, s1b_ref))
    for m_ref, acc_ref in streams:
        _init(m_ref, acc_ref)
    q = q_ref[...]
    n_plain = i // 2

    def fill(which, t):
        for sub in range(2):
            lanes = slice(sub * LANE, (sub + 1) * LANE)
            s = _dot_nt(k_ref[_step_keys(t), lanes], q[:, lanes])
            buffers[sub][which][...] = s
            smax_ref[2 * sub + which] = jnp.max(s, axis=0, keepdims=True)

    def consume(which, t, masked=False):
        if masked:
            ahead = (lax.broadcasted_iota(jnp.int32, (2 * T, T), 0)
                     - lax.broadcasted_iota(jnp.int32, (2 * T, T), 1))
            causal = ahead <= (i - 2 * t) * T
        for sub, (m_ref, acc_ref) in enumerate(streams):
            s = buffers[sub][which][...]
            col_max = smax_ref[2 * sub + which]
            if masked:
                s = jnp.where(causal, s, NEG_INF)
                col_max = None
            _softmax_step(s, _pair_values(vt_ref, sub, _step_keys(t)), m_ref, acc_ref, col_max)

    fill(0, 0)

    def body(u, carry):
        fill(1, 2 * u + 1)
        consume(0, 2 * u)
        fill(0, 2 * u + 2)
        consume(1, 2 * u + 1)
        return carry

    lax.fori_loop(0, n_plain // 2, body, 0)

    @pl.when(n_plain % 2 == 0)
    def _():
        consume(0, n_plain, masked=True)

    @pl.when(n_plain % 2 == 1)
    def _():
        fill(1, n_plain)
        consume(0, n_plain - 1)
        consume(1, n_plain, masked=True)

    o_ref[...] = _pair_output((acc0_ref, acc1_ref))


def _attn_bc(kernel, q_arr, k_arr, vt, extra, extra_specs, *, b, s, heads, width, q_block0,
             k_block0, name, score_buffers=0):
    nq = s // T
    pairs = heads // 2
    return pl.pallas_call(
        kernel,
        out_shape=jax.ShapeDtypeStruct((b * s, heads * HEAD_DIM), BF16),
        grid=(b, pairs, nq),
        in_specs=[
            pl.BlockSpec((T, width), lambda bi, hp, i: (bi * nq + i, q_block0 + hp)),
            pl.BlockSpec((s, width), lambda bi, hp, i: (bi, k_block0 + hp)),
            pl.BlockSpec((2 * VBC_ROWS, s), lambda bi, hp, i: (bi * pairs + hp, 0)),
            *extra_specs,
        ],
        out_specs=pl.BlockSpec((T, 2 * HEAD_DIM), lambda bi, hp, i: (bi * nq + i, hp)),
        scratch_shapes=[pltpu.VMEM((1, T), F32)] * 2 + [pltpu.VMEM((VBC_ROWS, T), F32)] * 2
        + [pltpu.VMEM((2 * T, T), F32)] * score_buffers
        + [pltpu.VMEM((score_buffers, 1, T), F32)] * (score_buffers > 0),
        compiler_params=_cparams("parallel", "parallel", "arbitrary"),
        name=name,
    )(q_arr, k_arr, vt, *extra)


def _merge_kernel(x_ref, g_ref, wg_ref, oa_ref, ob_ref, oc_ref, wa_ref, wb_ref, wc_ref,
                  wo_ref, o_ref):
    x = x_ref[...]
    d = x.shape[1]
    h = _rms(x, g_ref[...]).astype(BF16)
    gates = jax.nn.sigmoid(jnp.dot(h, wg_ref[...], preferred_element_type=F32))
    merged = None
    for n, (o_br, w_br) in enumerate(((oa_ref, wa_ref), (ob_ref, wb_ref), (oc_ref, wc_ref))):
        term = gates[:, n * d:(n + 1) * d] * jnp.dot(o_br[...], w_br[...],
                                                     preferred_element_type=F32)
        merged = term if merged is None else merged + term
    o_ref[...] = x + jnp.dot(merged.astype(BF16), wo_ref[...], preferred_element_type=F32)


def _merge(x2, g, wg, oa, ob, oc, wa, wb, wc, wo, *, tm):
    n, d = x2.shape
    row = lambda i: (i, 0)
    const = lambda i: (0, 0)
    return pl.pallas_call(
        _merge_kernel,
        out_shape=jax.ShapeDtypeStruct((n, d), F32),
        grid=(n // tm,),
        in_specs=[
            pl.BlockSpec((tm, d), row),
            pl.BlockSpec((1, d), const),
            _resident(wg.shape, const),
            pl.BlockSpec((tm, oa.shape[1]), row),
            pl.BlockSpec((tm, ob.shape[1]), row),
            pl.BlockSpec((tm, oc.shape[1]), row),
            _resident(wa.shape, const),
            _resident(wb.shape, const),
            _resident(wc.shape, const),
            _resident(wo.shape, const),
        ],
        out_specs=pl.BlockSpec((tm, d), row),
        compiler_params=_cparams("parallel"),
        name="merge",
    )(x2, g, wg, oa, ob, oc, wa, wb, wc, wo)


def _ffn_kernel(x_ref, g_ref, wg_ref, wu_ref, wd_ref, gf_ref, o_ref, *, n_chunks, final_norm):
    x = x_ref[...]
    h = _rms(x, g_ref[...]).astype(BF16)
    fc = wg_ref.shape[1] // n_chunks
    y = x
    for c in range(n_chunks):
        cols = slice(c * fc, (c + 1) * fc)
        gt = jnp.dot(h, wg_ref[:, cols], preferred_element_type=F32)
        up = jnp.dot(h, wu_ref[:, cols], preferred_element_type=F32)
        act = (gt * jax.nn.sigmoid(gt) * up).astype(BF16)
        y = y + jnp.dot(act, wd_ref[cols, :], preferred_element_type=F32)
    if final_norm:
        y = _rms(y, gf_ref[...])
    o_ref[...] = y


def _ffn(x2, g, wg, wu, wd, gf, *, tm, n_chunks, final_norm):
    n, d = x2.shape
    row = lambda i: (i, 0)
    const = lambda i: (0, 0)
    return pl.pallas_call(
        functools.partial(_ffn_kernel, n_chunks=n_chunks, final_norm=final_norm),
        out_shape=jax.ShapeDtypeStruct((n, d), F32),
        grid=(n // tm,),
        in_specs=[
            pl.BlockSpec((tm, d), row),
            pl.BlockSpec((1, d), const),
            _resident(wg.shape, const),
            _resident(wu.shape, const),
            _resident(wd.shape, const),
            pl.BlockSpec((1, d), const),
        ],
        out_specs=pl.BlockSpec((tm, d), row),
        compiler_params=_cparams("parallel"),
        name="ffn",
    )(x2, g, wg, wu, wd, gf)


def _pad_heads(w, heads, width, slot):
    d = w.shape[0]
    w = w.reshape(d, heads, width)
    return jnp.pad(w, ((0, 0), (0, 0), (0, slot - width))).reshape(d, heads * slot)


def _ones_col(heads, slot, at):
    col = np.zeros((heads, slot), np.float32)
    col[:, at] = 1.0
    return jnp.asarray(col.reshape(heads * slot, 1))


def _t5_bucket(rel):
    nb = T5_BUCKETS // 2
    max_exact = nb // 2
    n = jnp.abs(rel)
    nf = jnp.maximum(n, 1).astype(jnp.float32)
    large = max_exact + (jnp.log(nf / max_exact) / math.log(T5_MAX_DIST / max_exact)
                         * (nb - max_exact)).astype(jnp.int32)
    large = jnp.minimum(large, nb - 1)
    return jnp.where(rel > 0, nb, 0) + jnp.where(n < max_exact, n, large)


def _tile_rel(d_min, d_max):
    return ((T - 1 - jnp.arange(2 * T))[None, :]
            - jnp.arange(d_min, d_max + 1)[:, None] * T)


def _rel_vectors(table, idx):
    return jnp.moveaxis(table[idx], -1, 1).astype(F32)[:, :, None, :]


def kernel(x, norm_mix_g, w_in, b_forget, diff_lambda, diff_subln_g, t5_table, b_rel_table,
           w_br_a, w_br_b, w_br_c, w_out, norm_ffn_g, w_gate_up, w_down, final_norm_g):
    b, s, d = x.shape
    depth = w_in.shape[0]
    assert s % (B_QBLOCKS * T) == 0 and B_QBLOCKS % 2 == 0 and T // CHUNK == B_LEFT_CHUNKS
    n = b * s
    scale = HEAD_DIM ** -0.5 * LOG2E
    a_qk_w = A_HEADS * 2 * HEAD_DIM
    a_v_w = A_HEADS * A_VDIM
    bw = B_HEADS * HEAD_DIM
    cw = C_HEADS * HEAD_DIM
    d_ff = w_down.shape[1]
    x2 = x.reshape(n, d)

    t5_far = t5_table[T5_BUCKETS // 2 - 1].astype(F32)
    t5_tiles = _bias_tiles(
        _rel_vectors(t5_table, _t5_bucket(_tile_rel(-1, A_NEAR))) - t5_far[None, :, None, None],
        (2, 2, 2), _t5_distance, -1, band=False, name="bias_tiles_t5",
    ).reshape(A_HEADS, 2, 2, 2 * T, T)
    far_pieces, rest = [], t5_far * LOG2E
    for _ in range(C_PARTS):
        piece = rest.astype(BF16).astype(F32)
        far_pieces.append(piece)
        rest = rest - piece
    far_pieces = jnp.repeat(jnp.stack(far_pieces, axis=-1), 2, axis=0)
    spare = slice(HEAD_DIM, HEAD_DIM + C_PARTS)
    brow_qa = jnp.zeros((2 * A_HEADS, LANE), F32).at[:, spare].set(1.0)
    brow_ka = jnp.zeros((2 * A_HEADS, LANE), F32).at[:, spare].set(far_pieces)
    brow_qc = jnp.zeros((C_HEADS, LANE), F32).at[:, HEAD_DIM + C_PARTS:HEAD_DIM + 2 * C_PARTS].set(1.0)
    brow_kc = jnp.zeros((C_HEADS, LANE), F32).at[:, spare].set(1.0)
    b_slot0 = 4 * A_HEADS
    c_slot0 = b_slot0 + B_HEADS
    brow = jnp.concatenate([brow_qa.reshape(1, -1), brow_ka.reshape(1, -1),
                            jnp.zeros((1, 2 * B_HEADS * HEAD_DIM), F32),
                            brow_qc.reshape(1, -1), brow_kc.reshape(1, -1)], axis=1)
    band_idx = jnp.clip(_tile_rel(-1, 1), -B_REL_CLIP, CHUNK - 1) + B_REL_CLIP
    onea = _ones_col(A_HEADS, VA_ROWS, A_VDIM)
    onebc = _ones_col(B_HEADS, VBC_ROWS, HEAD_DIM)

    for l in range(depth):
        w = w_in[l]
        pieces, at = [], 0
        for width in (a_qk_w, a_qk_w, a_v_w, bw, bw, bw, cw, cw, cw, C_HEADS, N_BRANCH * d):
            pieces.append(w[:, at:at + width])
            at += width
        a_q, a_k, a_v, b_q, b_k, b_v, c_q, c_k, c_v, c_f, w_gates = pieces
        w_qk = jnp.concatenate([
            _pad_heads(a_q * scale, 2 * A_HEADS, HEAD_DIM, LANE),
            _pad_heads(a_k, 2 * A_HEADS, HEAD_DIM, LANE),
            b_q * scale,
            b_k,
            _pad_heads(c_q * scale, C_HEADS, HEAD_DIM, LANE),
            _pad_heads(c_k, C_HEADS, HEAD_DIM, LANE),
        ], axis=1).at[:, F_LANE0:F_LANE0 + C_HEADS].set(c_f).astype(BF16)
        wva = _pad_heads(a_v, A_HEADS, A_VDIM, VA_ROWS).T.astype(BF16)
        wvb = _pad_heads(b_v, B_HEADS, HEAD_DIM, VBC_ROWS).T.astype(BF16)
        wvc = _pad_heads(c_v, C_HEADS, HEAD_DIM, VBC_ROWS).T.astype(BF16)

        bf_row = jnp.zeros((1, LANE), F32).at[0, F_LANE0:F_LANE0 + C_HEADS].set(b_forget[l])
        qk, vta, vtb, vtc = _in_proj(
            x2, norm_mix_g[l][None], w_qk, brow, bf_row, wva, wvb, wvc, onea, onebc,
            b=b, tm=min(512, s), cq_tile=c_slot0 // C_HEADS)

        lam_init = 0.8 - 0.6 * math.exp(-0.3 * l)
        o_a = _attn_a(qk, vta, t5_tiles, diff_lambda[l].astype(F32),
                      diff_subln_g[l][:, None].astype(F32), b=b, s=s, lam_init=lam_init)

        band_tiles = _bias_tiles(
            _rel_vectors(b_rel_table[l], band_idx), (2, 2), _band_distance, -1, band=True,
            name="bias_tiles_band").reshape(B_HEADS, 2, 2 * T, T)
        o_b = _attn_b(qk, vtb, band_tiles, b=b, s=s, q_block0=b_slot0,
                      k_block0=b_slot0 + B_HEADS // 2)

        o_c = _attn_bc(_attn_c_kernel, qk, qk, vtc, (), (), b=b, s=s, heads=C_HEADS,
                       width=2 * LANE, q_block0=c_slot0 // 2, k_block0=(c_slot0 + C_HEADS) // 2,
                       name="attn_c", score_buffers=4)

        x2 = _merge(
            x2, norm_mix_g[l][None], w_gates.astype(BF16), o_a, o_b, o_c,
            w_br_a[l].astype(BF16), w_br_b[l].astype(BF16), w_br_c[l].astype(BF16),
            w_out[l].astype(BF16), tm=min(512, n))

        x2 = _ffn(
            x2, norm_ffn_g[l][None], w_gate_up[l][:, :d_ff].astype(BF16),
            w_gate_up[l][:, d_ff:].astype(BF16), w_down[l].astype(BF16),
            final_norm_g[None], tm=min(512, n), n_chunks=2, final_norm=(l == depth - 1))

    return x2.reshape(b, s, d)
```

```python
import functools
import math

import numpy as np
import jax
import jax.numpy as jnp
from jax import lax
from jax.experimental import pallas as pl
from jax.experimental.pallas import tpu as pltpu

CHUNK = 64
HEAD_DIM = 64
A_HEADS = 4
A_VDIM = 2 * HEAD_DIM
B_HEADS = 8
B_LEFT_CHUNKS = 8
B_REL_CLIP = 128
C_HEADS = 8
T5_BUCKETS = 32
T5_MAX_DIST = 2048
N_BRANCH = 3
RMS_EPS = 1e-6
NEG_INF = -1e30
LOG2E = math.log2(math.e)

LANE = 128
BF16_ROWS = 16
T = 512
A_NEAR = 3
VA_ROWS = A_VDIM + BF16_ROWS
VBC_ROWS = HEAD_DIM + BF16_ROWS
C_PARTS = 3
VMEM_LIMIT = 56 * 1024 * 1024

F32 = jnp.float32
BF16 = jnp.bfloat16


def _cparams(*sem):
    return pltpu.CompilerParams(dimension_semantics=sem, vmem_limit_bytes=VMEM_LIMIT)


def _resident(block_shape, index_map):
    return pl.BlockSpec(block_shape, index_map, pipeline_mode=pl.Buffered(1))


def _rms(x, g, axis=-1):
    y = x * lax.rsqrt(jnp.mean(x * x, axis=axis, keepdims=True) + RMS_EPS)
    return y * g


def _dot_nt(a, b):
    return lax.dot_general(a, b, (((1,), (1,)), ((), ())), preferred_element_type=F32)


F_LANE0 = HEAD_DIM + C_HEADS


def _bf16_pieces(x):
    pieces, rest = [], x
    for _ in range(C_PARTS):
        piece = rest.astype(BF16).astype(F32)
        pieces.append(piece)
        rest = rest - piece
    return pieces


def _in_proj_kernel(x_ref, g_ref, w_ref, brow_ref, bf_ref, tri_ref, pq_ref, pk_ref,
                    wva_ref, wvb_ref, wvc_ref, onea_ref, onebc_ref,
                    o_ref, va_ref, vb_ref, vc_ref, carry_ref, *, row_tiles_per_seq, cq_tile):
    h = _rms(x_ref[...], g_ref[...]).astype(BF16)
    va_ref[...] = (_dot_nt(wva_ref[...], h) + onea_ref[...]).astype(BF16)
    vb_ref[...] = (_dot_nt(wvb_ref[...], h) + onebc_ref[...]).astype(BF16)
    vc_ref[...] = (_dot_nt(wvc_ref[...], h) + onebc_ref[...]).astype(BF16)
    tn = pq_ref.shape[1]

    def tile(j):
        cols = slice(j * tn, (j + 1) * tn)
        return jnp.dot(h, w_ref[:, cols], preferred_element_type=F32) + brow_ref[:, cols]

    first = tile(0)
    o_ref[:, :tn] = first.astype(BF16)

    @pl.when(pl.program_id(0) % row_tiles_per_seq == 0)
    def _():
        carry_ref[...] = jnp.zeros_like(carry_ref)

    x = first[:, :LANE] + bf_ref[...]
    log_f = -(jnp.maximum(-x, 0.0) + jnp.log1p(jnp.exp(-jnp.abs(x))))
    pieces = jnp.concatenate([p.astype(BF16) for p in _bf16_pieces(log_f)], axis=1)
    sums = jnp.dot(tri_ref[...], pieces, preferred_element_type=F32)
    c = carry_ref[...]
    for p in range(C_PARTS):
        c = c + sums[:, p * LANE:(p + 1) * LANE]
    carry_ref[...] = c[c.shape[0] - 1:, :]

    is_gate = lax.broadcasted_iota(jnp.int32, c.shape, 1) // C_HEADS == F_LANE0 // C_HEADS
    packed = None
    for p, piece in enumerate(_bf16_pieces(c * LOG2E)):
        piece = jnp.where(is_gate, piece, 0.0)
        piece = piece if p == 0 else pltpu.roll(piece, p * C_HEADS, 1)
        packed = piece if packed is None else packed + piece
    packed = packed.astype(BF16)

    for j in range(1, w_ref.shape[1] // tn):
        acc = tile(j)
        if j == cq_tile:
            acc = acc + jnp.dot(packed, pq_ref[...], preferred_element_type=F32)
        if j == cq_tile + 1:
            acc = acc - jnp.dot(packed, pk_ref[...], preferred_element_type=F32)
        o_ref[:, j * tn:(j + 1) * tn] = acc.astype(BF16)


def _placement_constants():
    w = C_HEADS * LANE
    pq = np.zeros((LANE, w), np.float32)
    pk = np.zeros((LANE, w), np.float32)
    for h in range(C_HEADS):
        for p in range(C_PARTS):
            pq[F_LANE0 + p * C_HEADS + h, h * LANE + HEAD_DIM + p] = 1.0
            pk[F_LANE0 + p * C_HEADS + h, h * LANE + HEAD_DIM + C_PARTS + p] = 1.0
    return jnp.asarray(pq, BF16), jnp.asarray(pk, BF16)


def _in_proj(x2, g, w_qk, brow, bf_row, wva, wvb, wvc, onea, onebc, *, b, tm, cq_tile):
    n, d = x2.shape
    s = n // b
    nt = s // tm
    ncol = w_qk.shape[1]
    ra, rbc = wva.shape[0], wvb.shape[0]
    tri = jnp.asarray(np.tril(np.ones((tm, tm), np.float32)), BF16)
    pq, pk = _placement_constants()
    const = lambda i: (0, 0)
    vt_map = lambda i: (i // nt, i % nt)
    return pl.pallas_call(
        functools.partial(_in_proj_kernel, row_tiles_per_seq=nt, cq_tile=cq_tile),
        out_shape=(jax.ShapeDtypeStruct((n, ncol), BF16),
                   jax.ShapeDtypeStruct((b * ra, s), BF16), jax.ShapeDtypeStruct((b * rbc, s), BF16),
                   jax.ShapeDtypeStruct((b * rbc, s), BF16)),
        grid=(n // tm,),
        in_specs=[
            pl.BlockSpec((tm, d), lambda i: (i, 0)),
            pl.BlockSpec((1, d), const),
            _resident((d, ncol), const),
            pl.BlockSpec((1, ncol), const),
            pl.BlockSpec((1, LANE), const),
            _resident((tm, tm), const),
            _resident(pq.shape, const),
            _resident(pk.shape, const),
            _resident((ra, d), const),
            _resident((rbc, d), const),
            _resident((rbc, d), const),
            pl.BlockSpec((ra, 1), const),
            pl.BlockSpec((rbc, 1), const),
        ],
        out_specs=(
            pl.BlockSpec((tm, ncol), lambda i: (i, 0)),
            pl.BlockSpec((ra, tm), vt_map),
            pl.BlockSpec((rbc, tm), vt_map),
            pl.BlockSpec((rbc, tm), vt_map),
        ),
        scratch_shapes=[pltpu.VMEM((1, LANE), F32)],
        compiler_params=_cparams("arbitrary"),
        name="in_proj",
    )(x2, g, w_qk, brow, bf_row, tri, pq, pk, wva, wvb, wvc, onea, onebc)


def _bias_tile_kernel(x_ref, o_ref, *, distance, band):
    d = distance(*(pl.program_id(a) for a in range(len(o_ref.shape) - 2)))
    x = jnp.broadcast_to(x_ref[0, 0], (T, 2 * T))
    tile = pltpu.roll(x, T + 1, 1, stride=1, stride_axis=0)[:, :T] * LOG2E
    ahead = (lax.broadcasted_iota(jnp.int32, (T, T), 0) // CHUNK
             - lax.broadcasted_iota(jnp.int32, (T, T), 1) // CHUNK
             - d * (T // CHUNK))
    if band:
        tile = jnp.where(ahead >= -B_LEFT_CHUNKS, tile, NEG_INF)
    o_ref[...] = jnp.where(ahead <= 0, tile, NEG_INF).reshape(o_ref.shape)


def _bias_tiles(x, lead, distance, d_min, *, band, name):
    heads = x.shape[1]
    zeros = (0,) * 2
    return pl.pallas_call(
        functools.partial(_bias_tile_kernel, distance=distance, band=band),
        out_shape=jax.ShapeDtypeStruct((heads, *lead, T, T), F32),
        grid=(heads, *lead),
        in_specs=[pl.BlockSpec((1, 1, 1, 2 * T),
                               lambda h, *idx: (distance(h, *idx) - d_min, h, 0, 0))],
        out_specs=pl.BlockSpec((1,) * (1 + len(lead)) + (T, T), lambda h, *idx: (h, *idx, *zeros)),
        compiler_params=_cparams(*(("parallel",) * (1 + len(lead)))),
        name=name,
    )(x)


def _t5_distance(h, parity, which, half):
    return parity + 2 * (1 - which) - half


def _band_distance(h, first, half):
    return 1 - half - first


def _softmax_step(s, vt, m_ref, acc_ref, col_max=None):
    rows = vt.shape[0]
    m_prev = m_ref[...]
    if col_max is None:
        col_max = jnp.max(s, axis=0, keepdims=True)
    m_new = jnp.maximum(m_prev, col_max)
    alpha = jnp.exp2(m_prev - m_new)
    p = jnp.exp2(s - m_new).astype(BF16)
    acc_ref[:rows, :] = acc_ref[:rows, :] * alpha + jnp.dot(vt, p, preferred_element_type=F32)
    m_ref[...] = m_new


def _keys(j, n_blocks=1):
    return pl.ds(pl.multiple_of(j * T, T), n_blocks * T)


def _step_keys(t):
    return _keys(2 * t, 2)


def _init(m_ref, acc_ref):
    m_ref[...] = jnp.full_like(m_ref, NEG_INF)
    acc_ref[...] = jnp.zeros_like(acc_ref)


def _attn_a_kernel(q_ref, k_ref, vt_ref, tile_ref, lp_ref, sg_ref, o_ref,
                   m0_ref, m1_ref, acc0_ref, acc1_ref, s0a_ref, s0b_ref, s1a_ref, s1b_ref,
                   smax_ref, *, lam_init):
    i = pl.program_id(2)
    streams = ((m0_ref, acc0_ref), (m1_ref, acc1_ref))
    buffers = ((s0a_ref, s0b_ref), (s1a_ref, s1b_ref))
    for m_ref, acc_ref in streams:
        _init(m_ref, acc_ref)
    q = q_ref[...]
    parity = i % 2
    n_steps = i // 2 + 1
    n_far = jnp.maximum(n_steps - 2, 0)

    def fill(which, t):
        for sub in range(2):
            lanes = slice(sub * LANE, (sub + 1) * LANE)
            s = _dot_nt(k_ref[_step_keys(t), lanes], q[:, lanes])
            buffers[sub][which][...] = s
            smax_ref[2 * sub + which] = jnp.max(s, axis=0, keepdims=True)

    def consume(which, t, near=None, par=None):
        n_blocks = 1 if (near == 1 and par == 0) else 2
        vt = vt_ref[:, _keys(2 * t, n_blocks)]
        for sub, (m_ref, acc_ref) in enumerate(streams):
            s = buffers[sub][which][:n_blocks * T, :]
            col_max = smax_ref[2 * sub + which]
            if near is not None:
                s = s + tile_ref[0, par, near, :n_blocks * T, :]
                col_max = None
            _softmax_step(s, vt, m_ref, acc_ref, col_max)

    fill(0, 0)

    def far_body(u, carry):
        fill(1, 2 * u + 1)
        consume(0, 2 * u)
        fill(0, 2 * u + 2)
        consume(1, 2 * u + 1)
        return carry

    lax.fori_loop(0, n_far // 2, far_body, 0)

    for par in range(2):
        @pl.when(jnp.logical_and(n_steps == 1, parity == par))
        def _(par=par):
            consume(0, 0, 1, par)

        @pl.when(jnp.logical_and(jnp.logical_and(n_steps >= 2, n_far % 2 == 0), parity == par))
        def _(par=par):
            fill(1, n_steps - 1)
            consume(0, n_steps - 2, 0, par)
            consume(1, n_steps - 1, 1, par)

        @pl.when(jnp.logical_and(n_far % 2 == 1, parity == par))
        def _(par=par):
            fill(1, n_far)
            consume(0, n_far - 1)
            fill(0, n_far + 1)
            consume(1, n_far, 0, par)
            consume(0, n_far + 1, 1, par)

    lp = lp_ref[...]
    lam = (jnp.exp(jnp.sum(lp[0:1] * lp[1:2], axis=-1, keepdims=True))
           - jnp.exp(jnp.sum(lp[2:3] * lp[3:4], axis=-1, keepdims=True)) + lam_init)
    acc0 = acc0_ref[...]
    acc1 = acc1_ref[...]
    ot = (acc0[:A_VDIM] / acc0[A_VDIM:A_VDIM + 1]
          - lam * (acc1[:A_VDIM] / acc1[A_VDIM:A_VDIM + 1]))
    ot = _rms(ot, sg_ref[...], axis=0) * (1.0 - lam_init)
    o_ref[...] = ot.T.astype(BF16)


def _attn_a(qk, vt, tiles, lp, sg, *, b, s, lam_init):
    nq = s // T
    wa = 2 * LANE
    return pl.pallas_call(
        functools.partial(_attn_a_kernel, lam_init=lam_init),
        out_shape=jax.ShapeDtypeStruct((b * s, A_HEADS * A_VDIM), BF16),
        grid=(b, A_HEADS, nq),
        in_specs=[
            pl.BlockSpec((T, wa), lambda bi, h, i: (bi * nq + i, h)),
            pl.BlockSpec((s, wa), lambda bi, h, i: (bi, A_HEADS + h)),
            pl.BlockSpec((VA_ROWS, s), lambda bi, h, i: (bi * A_HEADS + h, 0)),
            _resident((1, 2, 2, 2 * T, T), lambda bi, h, i: (h, 0, 0, 0, 0)),
            pl.BlockSpec((4, HEAD_DIM), lambda bi, h, i: (0, 0)),
            pl.BlockSpec((A_VDIM, 1), lambda bi, h, i: (0, 0)),
        ],
        out_specs=pl.BlockSpec((T, A_VDIM), lambda bi, h, i: (bi * nq + i, h)),
        scratch_shapes=[pltpu.VMEM((1, T), F32), pltpu.VMEM((1, T), F32),
                        pltpu.VMEM((VA_ROWS, T), F32), pltpu.VMEM((VA_ROWS, T), F32)]
        + [pltpu.VMEM((2 * T, T), F32)] * 4 + [pltpu.VMEM((4, 1, T), F32)],
        compiler_params=_cparams("parallel", "parallel", "arbitrary"),
        name="attn_a",
    )(qk, qk, vt, tiles, lp, sg)


def _pair_output(acc_refs):
    halves = []
    for acc_ref in acc_refs:
        acc = acc_ref[...]
        halves.append(acc[:HEAD_DIM] / acc[HEAD_DIM:HEAD_DIM + 1])
    return jnp.concatenate(halves, axis=0).T.astype(BF16)


def _pair_values(vt_ref, sub, keys):
    return vt_ref[sub * VBC_ROWS:(sub + 1) * VBC_ROWS, keys]


B_QBLOCKS = 4


def _attn_b_kernel(q_ref, k_ref, vt_ref, tile_ref, o_ref, s0a_ref, s0b_ref, s1a_ref, s1b_ref,
                   smax_ref):
    buffers = ((s0a_ref, s0b_ref), (s1a_ref, s1b_ref))
    lane_head = lax.broadcasted_iota(jnp.int32, (T, LANE), 1) // HEAD_DIM

    def keys_of(r):
        g = pl.program_id(2) * B_QBLOCKS + r
        return g, _keys(jnp.maximum(g - 1, 0), 2)

    def fill(which, r):
        g, keys = keys_of(r)
        first = (g == 0).astype(jnp.int32)
        k = k_ref[keys, :]
        q = q_ref[r * T:(r + 1) * T, :]
        for sub in range(2):
            q_head = jnp.where(lane_head == sub, q, jnp.zeros_like(q))
            s = _dot_nt(k, q_head) + tile_ref[sub, first]
            buffers[sub][which][...] = s
            smax_ref[2 * sub + which] = jnp.max(s, axis=0, keepdims=True)

    def consume(which, r):
        _, keys = keys_of(r)
        halves = []
        for sub in range(2):
            p = jnp.exp2(buffers[sub][which][...] - smax_ref[2 * sub + which]).astype(BF16)
            acc = jnp.dot(_pair_values(vt_ref, sub, keys), p, preferred_element_type=F32)
            halves.append(acc[:HEAD_DIM] / acc[HEAD_DIM:HEAD_DIM + 1])
        o_ref[r * T:(r + 1) * T, :] = jnp.concatenate(halves, axis=0).T.astype(BF16)

    fill(0, 0)
    for r in range(B_QBLOCKS):
        if r + 1 < B_QBLOCKS:
            fill((r + 1) % 2, r + 1)
        consume(r % 2, r)


def _attn_b(qk, vt, tiles, *, b, s, q_block0, k_block0):
    groups = s // (B_QBLOCKS * T)
    pairs = B_HEADS // 2
    return pl.pallas_call(
        _attn_b_kernel,
        out_shape=jax.ShapeDtypeStruct((b * s, B_HEADS * HEAD_DIM), BF16),
        grid=(b, pairs, groups),
        in_specs=[
            pl.BlockSpec((B_QBLOCKS * T, LANE), lambda bi, hp, i: (bi * groups + i, q_block0 + hp)),
            pl.BlockSpec((s, LANE), lambda bi, hp, i: (bi, k_block0 + hp)),
            pl.BlockSpec((2 * VBC_ROWS, s), lambda bi, hp, i: (bi * pairs + hp, 0)),
            pl.BlockSpec((2, 2, 2 * T, T), lambda bi, hp, i: (hp, 0, 0, 0)),
        ],
        out_specs=pl.BlockSpec((B_QBLOCKS * T, 2 * HEAD_DIM),
                               lambda bi, hp, i: (bi * groups + i, hp)),
        scratch_shapes=[pltpu.VMEM((2 * T, T), F32)] * 4 + [pltpu.VMEM((4, 1, T), F32)],
        compiler_params=_cparams("parallel", "parallel", "arbitrary"),
        name="attn_b",
    )(qk, qk, vt, tiles)


def _attn_c_kernel(q_ref, k_ref, vt_ref, o_ref, m0_ref, m1_ref, acc0_ref, acc1_ref,
                   s0a_ref, s0b_ref, s1a_ref, s1b_ref, smax_ref):
    i = pl.program_id(2)
    streams = ((m0_ref, acc0_ref), (m1_ref, acc1_ref))
    buffers = ((s0a_ref, s0b_ref), (s1a_ref, s1b_ref))
    for m_ref, acc_ref in streams:
        _init(m_ref, acc_ref)
    q = q_ref[...]
    n_plain = i // 2

    def fill(which, t):
        for sub in range(2):
            lanes = slice(sub * LANE, (sub + 1) * LANE)
            s = _dot_nt(k_ref[_step_keys(t), lanes], q[:, lanes])
            buffers[sub][which][...] = s
            smax_ref[2 * sub + which] = jnp.max(s, axis=0, keepdims=True)

    def consume(which, t, diagonal_at=None):
        n_blocks = 2 if diagonal_at is None else diagonal_at + 1
        keys = _keys(2 * t, n_blocks)
        if diagonal_at is not None:
            ahead = (lax.broadcasted_iota(jnp.int32, (n_blocks * T, T), 0)
                     - lax.broadcasted_iota(jnp.int32, (n_blocks * T, T), 1))
            causal = ahead <= diagonal_at * T
        for sub, (m_ref, acc_ref) in enumerate(streams):
            s = buffers[sub][which][:n_blocks * T, :]
            col_max = smax_ref[2 * sub + which]
            if diagonal_at is not None:
                s = jnp.where(causal, s, NEG_INF)
                col_max = None
            _softmax_step(s, _pair_values(vt_ref, sub, keys), m_ref, acc_ref, col_max)

    fill(0, 0)

    def body(u, carry):
        fill(1, 2 * u + 1)
        consume(0, 2 * u)
        fill(0, 2 * u + 2)
        consume(1, 2 * u + 1)
        return carry

    lax.fori_loop(0, n_plain // 2, body, 0)

    for diagonal_at in range(2):
        @pl.when(jnp.logical_and(n_plain % 2 == 0, i % 2 == diagonal_at))
        def _(diagonal_at=diagonal_at):
            consume(0, n_plain, diagonal_at)

        @pl.when(jnp.logical_and(n_plain % 2 == 1, i % 2 == diagonal_at))
        def _(diagonal_at=diagonal_at):
            fill(1, n_plain)
            consume(0, n_plain - 1)
            consume(1, n_plain, diagonal_at)

    o_ref[...] = _pair_output((acc0_ref, acc1_ref))


def _attn_bc(kernel, q_arr, k_arr, vt, extra, extra_specs, *, b, s, heads, width, q_block0,
             k_block0, name, score_buffers=0):
    nq = s // T
    pairs = heads // 2
    return pl.pallas_call(
        kernel,
        out_shape=jax.ShapeDtypeStruct((b * s, heads * HEAD_DIM), BF16),
        grid=(b, pairs, nq),
        in_specs=[
            pl.BlockSpec((T, width), lambda bi, hp, i: (bi * nq + i, q_block0 + hp)),
            pl.BlockSpec((s, width), lambda bi, hp, i: (bi, k_block0 + hp)),
            pl.BlockSpec((2 * VBC_ROWS, s), lambda bi, hp, i: (bi * pairs + hp, 0)),
            *extra_specs,
        ],
        out_specs=pl.BlockSpec((T, 2 * HEAD_DIM), lambda bi, hp, i: (bi * nq + i, hp)),
        scratch_shapes=[pltpu.VMEM((1, T), F32)] * 2 + [pltpu.VMEM((VBC_ROWS, T), F32)] * 2
        + [pltpu.VMEM((2 * T, T), F32)] * score_buffers
        + [pltpu.VMEM((score_buffers, 1, T), F32)] * (score_buffers > 0),
        compiler_params=_cparams("parallel", "parallel", "arbitrary"),
        name=name,
    )(q_arr, k_arr, vt, *extra)


def _merge_kernel(x_ref, g_ref, wg_ref, oa_ref, ob_ref, oc_ref, wa_ref, wb_ref, wc_ref,
                  wo_ref, o_ref):
    x = x_ref[...]
    d = x.shape[1]
    h = _rms(x, g_ref[...]).astype(BF16)
    gates = jax.nn.sigmoid(jnp.dot(h, wg_ref[...], preferred_element_type=F32))
    merged = None
    for n, (o_br, w_br) in enumerate(((oa_ref, wa_ref), (ob_ref, wb_ref), (oc_ref, wc_ref))):
        term = gates[:, n * d:(n + 1) * d] * jnp.dot(o_br[...], w_br[...],
                                                     preferred_element_type=F32)
        merged = term if merged is None else merged + term
    o_ref[...] = x + jnp.dot(merged.astype(BF16), wo_ref[...], preferred_element_type=F32)


def _merge(x2, g, wg, oa, ob, oc, wa, wb, wc, wo, *, tm):
    n, d = x2.shape
    row = lambda i: (i, 0)
    const = lambda i: (0, 0)
    return pl.pallas_call(
        _merge_kernel,
        out_shape=jax.ShapeDtypeStruct((n, d), F32),
        grid=(n // tm,),
        in_specs=[
            pl.BlockSpec((tm, d), row),
            pl.BlockSpec((1, d), const),
            _resident(wg.shape, const),
            pl.BlockSpec((tm, oa.shape[1]), row),
            pl.BlockSpec((tm, ob.shape[1]), row),
            pl.BlockSpec((tm, oc.shape[1]), row),
            _resident(wa.shape, const),
            _resident(wb.shape, const),
            _resident(wc.shape, const),
            _resident(wo.shape, const),
        ],
        out_specs=pl.BlockSpec((tm, d), row),
        compiler_params=_cparams("parallel"),
        name="merge",
    )(x2, g, wg, oa, ob, oc, wa, wb, wc, wo)


def _ffn_kernel(x_ref, g_ref, wg_ref, wu_ref, wd_ref, gf_ref, o_ref, *, n_chunks, final_norm):
    x = x_ref[...]
    h = _rms(x, g_ref[...]).astype(BF16)
    fc = wg_ref.shape[1] // n_chunks
    y = x
    for c in range(n_chunks):
        cols = slice(c * fc, (c + 1) * fc)
        gt = jnp.dot(h, wg_ref[:, cols], preferred_element_type=F32)
        up = jnp.dot(h, wu_ref[:, cols], preferred_element_type=F32)
        act = (gt * jax.nn.sigmoid(gt) * up).astype(BF16)
        y = y + jnp.dot(act, wd_ref[cols, :], preferred_element_type=F32)
    if final_norm:
        y = _rms(y, gf_ref[...])
    o_ref[...] = y


def _ffn(x2, g, wg, wu, wd, gf, *, tm, n_chunks, final_norm):
    n, d = x2.shape
    row = lambda i: (i, 0)
    const = lambda i: (0, 0)
    return pl.pallas_call(
        functools.partial(_ffn_kernel, n_chunks=n_chunks, final_norm=final_norm),
        out_shape=jax.ShapeDtypeStruct((n, d), F32),
        grid=(n // tm,),
        in_specs=[
            pl.BlockSpec((tm, d), row),
            pl.BlockSpec((1, d), const),
            _resident(wg.shape, const),
            _resident(wu.shape, const),
            _resident(wd.shape, const),
            pl.BlockSpec((1, d), const),
        ],
        out_specs=pl.BlockSpec((tm, d), row),
        compiler_params=_cparams("parallel"),
        name="ffn",
    )(x2, g, wg, wu, wd, gf)


def _pad_heads(w, heads, width, slot):
    d = w.shape[0]
    w = w.reshape(d, heads, width)
    return jnp.pad(w, ((0, 0), (0, 0), (0, slot - width))).reshape(d, heads * slot)


def _ones_col(heads, slot, at):
    col = np.zeros((heads, slot), np.float32)
    col[:, at] = 1.0
    return jnp.asarray(col.reshape(heads * slot, 1))


def _t5_bucket(rel):
    nb = T5_BUCKETS // 2
    max_exact = nb // 2
    n = jnp.abs(rel)
    nf = jnp.maximum(n, 1).astype(jnp.float32)
    large = max_exact + (jnp.log(nf / max_exact) / math.log(T5_MAX_DIST / max_exact)
                         * (nb - max_exact)).astype(jnp.int32)
    large = jnp.minimum(large, nb - 1)
    return jnp.where(rel > 0, nb, 0) + jnp.where(n < max_exact, n, large)


def _tile_rel(d_min, d_max):
    return ((T - 1 - jnp.arange(2 * T))[None, :]
            - jnp.arange(d_min, d_max + 1)[:, None] * T)


def _rel_vectors(table, idx):
    return jnp.moveaxis(table[idx], -1, 1).astype(F32)[:, :, None, :]


def kernel(x, norm_mix_g, w_in, b_forget, diff_lambda, diff_subln_g, t5_table, b_rel_table,
           w_br_a, w_br_b, w_br_c, w_out, norm_ffn_g, w_gate_up, w_down, final_norm_g):
    b, s, d = x.shape
    depth = w_in.shape[0]
    assert s % (B_QBLOCKS * T) == 0 and B_QBLOCKS % 2 == 0 and T // CHUNK == B_LEFT_CHUNKS
    n = b * s
    scale = HEAD_DIM ** -0.5 * LOG2E
    a_qk_w = A_HEADS * 2 * HEAD_DIM
    a_v_w = A_HEADS * A_VDIM
    bw = B_HEADS * HEAD_DIM
    cw = C_HEADS * HEAD_DIM
    d_ff = w_down.shape[1]
    x2 = x.reshape(n, d)

    t5_far = t5_table[T5_BUCKETS // 2 - 1].astype(F32)
    t5_tiles = _bias_tiles(
        _rel_vectors(t5_table, _t5_bucket(_tile_rel(-1, A_NEAR))) - t5_far[None, :, None, None],
        (2, 2, 2), _t5_distance, -1, band=False, name="bias_tiles_t5",
    ).reshape(A_HEADS, 2, 2, 2 * T, T)
    far_pieces, rest = [], t5_far * LOG2E
    for _ in range(C_PARTS):
        piece = rest.astype(BF16).astype(F32)
        far_pieces.append(piece)
        rest = rest - piece
    far_pieces = jnp.repeat(jnp.stack(far_pieces, axis=-1), 2, axis=0)
    spare = slice(HEAD_DIM, HEAD_DIM + C_PARTS)
    brow_qa = jnp.zeros((2 * A_HEADS, LANE), F32).at[:, spare].set(1.0)
    brow_ka = jnp.zeros((2 * A_HEADS, LANE), F32).at[:, spare].set(far_pieces)
    brow_qc = jnp.zeros((C_HEADS, LANE), F32).at[:, HEAD_DIM + C_PARTS:HEAD_DIM + 2 * C_PARTS].set(1.0)
    brow_kc = jnp.zeros((C_HEADS, LANE), F32).at[:, spare].set(1.0)
    b_slot0 = 4 * A_HEADS
    c_slot0 = b_slot0 + B_HEADS
    brow = jnp.concatenate([brow_qa.reshape(1, -1), brow_ka.reshape(1, -1),
                            jnp.zeros((1, 2 * B_HEADS * HEAD_DIM), F32),
                            brow_qc.reshape(1, -1), brow_kc.reshape(1, -1)], axis=1)
    band_idx = jnp.clip(_tile_rel(-1, 1), -B_REL_CLIP, CHUNK - 1) + B_REL_CLIP
    onea = _ones_col(A_HEADS, VA_ROWS, A_VDIM)
    onebc = _ones_col(B_HEADS, VBC_ROWS, HEAD_DIM)

    for l in range(depth):
        w = w_in[l]
        pieces, at = [], 0
        for width in (a_qk_w, a_qk_w, a_v_w, bw, bw, bw, cw, cw, cw, C_HEADS, N_BRANCH * d):
            pieces.append(w[:, at:at + width])
            at += width
        a_q, a_k, a_v, b_q, b_k, b_v, c_q, c_k, c_v, c_f, w_gates = pieces
        w_qk = jnp.concatenate([
            _pad_heads(a_q * scale, 2 * A_HEADS, HEAD_DIM, LANE),
            _pad_heads(a_k, 2 * A_HEADS, HEAD_DIM, LANE),
            b_q * scale,
            b_k,
            _pad_heads(c_q * scale, C_HEADS, HEAD_DIM, LANE),
            _pad_heads(c_k, C_HEADS, HEAD_DIM, LANE),
        ], axis=1).at[:, F_LANE0:F_LANE0 + C_HEADS].set(c_f).astype(BF16)
        wva = _pad_heads(a_v, A_HEADS, A_VDIM, VA_ROWS).T.astype(BF16)
        wvb = _pad_heads(b_v, B_HEADS, HEAD_DIM, VBC_ROWS).T.astype(BF16)
        wvc = _pad_heads(c_v, C_HEADS, HEAD_DIM, VBC_ROWS).T.astype(BF16)

        bf_row = jnp.zeros((1, LANE), F32).at[0, F_LANE0:F_LANE0 + C_HEADS].set(b_forget[l])
        qk, vta, vtb, vtc = _in_proj(
            x2, norm_mix_g[l][None], w_qk, brow, bf_row, wva, wvb, wvc, onea, onebc,
            b=b, tm=min(512, s), cq_tile=c_slot0 // C_HEADS)

        lam_init = 0.8 - 0.6 * math.exp(-0.3 * l)
        o_a = _attn_a(qk, vta, t5_tiles, diff_lambda[l].astype(F32),
                      diff_subln_g[l][:, None].astype(F32), b=b, s=s, lam_init=lam_init)

        band_tiles = _bias_tiles(
            _rel_vectors(b_rel_table[l], band_idx), (2, 2), _band_distance, -1, band=True,
            name="bias_tiles_band").reshape(B_HEADS, 2, 2 * T, T)
        o_b = _attn_b(qk, vtb, band_tiles, b=b, s=s, q_block0=b_slot0,
                      k_block0=b_slot0 + B_HEADS // 2)

        o_c = _attn_bc(_attn_c_kernel, qk, qk, vtc, (), (), b=b, s=s, heads=C_HEADS,
                       width=2 * LANE, q_block0=c_slot0 // 2, k_block0=(c_slot0 + C_HEADS) // 2,
                       name="attn_c", score_buffers=4)

        x2 = _merge(
            x2, norm_mix_g[l][None], w_gates.astype(BF16), o_a, o_b, o_c,
            w_br_a[l].astype(BF16), w_br_b[l].astype(BF16), w_br_c[l].astype(BF16),
            w_out[l].astype(BF16), tm=min(512, n))

        x2 = _ffn(
            x2, norm_ffn_g[l][None], w_gate_up[l][:, :d_ff].astype(BF16),
            w_gate_up[l][:, d_ff:].astype(BF16), w_down[l].astype(BF16),
            final_norm_g[None], tm=min(512, n), n_chunks=2, final_norm=(l == depth - 1))

    return x2.reshape(b, s, d)
```

```python
import functools
import math

import numpy as np
import jax
import jax.numpy as jnp
from jax import lax
from jax.experimental import pallas as pl
from jax.experimental.pallas import tpu as pltpu

CHUNK = 64
HEAD_DIM = 64
A_HEADS = 4
A_VDIM = 2 * HEAD_DIM
B_HEADS = 8
B_LEFT_CHUNKS = 8
B_REL_CLIP = 128
C_HEADS = 8
T5_BUCKETS = 32
T5_MAX_DIST = 2048
N_BRANCH = 3
RMS_EPS = 1e-6
NEG_INF = -1e30
LOG2E = math.log2(math.e)

LANE = 128
BF16_ROWS = 16
T = 512
A_NEAR = 3
VA_ROWS = A_VDIM + BF16_ROWS
VBC_ROWS = HEAD_DIM + BF16_ROWS
C_PARTS = 3
VMEM_LIMIT = 56 * 1024 * 1024

F32 = jnp.float32
BF16 = jnp.bfloat16


def _cparams(*sem):
    return pltpu.CompilerParams(dimension_semantics=sem, vmem_limit_bytes=VMEM_LIMIT)


def _resident(block_shape, index_map):
    return pl.BlockSpec(block_shape, index_map, pipeline_mode=pl.Buffered(1))


def _rms(x, g, axis=-1):
    y = x * lax.rsqrt(jnp.mean(x * x, axis=axis, keepdims=True) + RMS_EPS)
    return y * g


def _dot_nt(a, b):
    return lax.dot_general(a, b, (((1,), (1,)), ((), ())), preferred_element_type=F32)


F_LANE0 = HEAD_DIM + C_HEADS


def _bf16_pieces(x):
    pieces, rest = [], x
    for _ in range(C_PARTS):
        piece = rest.astype(BF16).astype(F32)
        pieces.append(piece)
        rest = rest - piece
    return pieces


def _in_proj_kernel(x_ref, g_ref, w_ref, brow_ref, bf_ref, tri_ref, pq_ref, pk_ref,
                    wva_ref, wvb_ref, wvc_ref, onea_ref, onebc_ref,
                    o_ref, va_ref, vb_ref, vc_ref, carry_ref, *, row_tiles_per_seq, cq_tile):
    h = _rms(x_ref[...], g_ref[...]).astype(BF16)
    va_ref[...] = (_dot_nt(wva_ref[...], h) + onea_ref[...]).astype(BF16)
    vb_ref[...] = (_dot_nt(wvb_ref[...], h) + onebc_ref[...]).astype(BF16)
    vc_ref[...] = (_dot_nt(wvc_ref[...], h) + onebc_ref[...]).astype(BF16)
    tn = pq_ref.shape[1]

    def tile(j):
        cols = slice(j * tn, (j + 1) * tn)
        return jnp.dot(h, w_ref[:, cols], preferred_element_type=F32) + brow_ref[:, cols]

    first = tile(0)
    o_ref[:, :tn] = first.astype(BF16)

    @pl.when(pl.program_id(0) % row_tiles_per_seq == 0)
    def _():
        carry_ref[...] = jnp.zeros_like(carry_ref)

    x = first[:, :LANE] + bf_ref[...]
    log_f = -(jnp.maximum(-x, 0.0) + jnp.log1p(jnp.exp(-jnp.abs(x))))
    pieces = jnp.concatenate([p.astype(BF16) for p in _bf16_pieces(log_f)], axis=1)
    sums = jnp.dot(tri_ref[...], pieces, preferred_element_type=F32)
    c = carry_ref[...]
    for p in range(C_PARTS):
        c = c + sums[:, p * LANE:(p + 1) * LANE]
    carry_ref[...] = c[c.shape[0] - 1:, :]

    is_gate = lax.broadcasted_iota(jnp.int32, c.shape, 1) // C_HEADS == F_LANE0 // C_HEADS
    packed = None
    for p, piece in enumerate(_bf16_pieces(c * LOG2E)):
        piece = jnp.where(is_gate, piece, 0.0)
        piece = piece if p == 0 else pltpu.roll(piece, p * C_HEADS, 1)
        packed = piece if packed is None else packed + piece
    packed = packed.astype(BF16)

    for j in range(1, w_ref.shape[1] // tn):
        acc = tile(j)
        if j == cq_tile:
            acc = acc + jnp.dot(packed, pq_ref[...], preferred_element_type=F32)
        if j == cq_tile + 1:
            acc = acc - jnp.dot(packed, pk_ref[...], preferred_element_type=F32)
        o_ref[:, j * tn:(j + 1) * tn] = acc.astype(BF16)


def _placement_constants():
    w = C_HEADS * LANE
    pq = np.zeros((LANE, w), np.float32)
    pk = np.zeros((LANE, w), np.float32)
    for h in range(C_HEADS):
        for p in range(C_PARTS):
            pq[F_LANE0 + p * C_HEADS + h, h * LANE + HEAD_DIM + p] = 1.0
            pk[F_LANE0 + p * C_HEADS + h, h * LANE + HEAD_DIM + C_PARTS + p] = 1.0
    return jnp.asarray(pq, BF16), jnp.asarray(pk, BF16)


def _in_proj(x2, g, w_qk, brow, bf_row, wva, wvb, wvc, onea, onebc, *, b, tm, cq_tile):
    n, d = x2.shape
    s = n // b
    nt = s // tm
    ncol = w_qk.shape[1]
    ra, rbc = wva.shape[0], wvb.shape[0]
    tri = jnp.asarray(np.tril(np.ones((tm, tm), np.float32)), BF16)
    pq, pk = _placement_constants()
    const = lambda i: (0, 0)
    vt_map = lambda i: (i // nt, i % nt)
    return pl.pallas_call(
        functools.partial(_in_proj_kernel, row_tiles_per_seq=nt, cq_tile=cq_tile),
        out_shape=(jax.ShapeDtypeStruct((n, ncol), BF16),
                   jax.ShapeDtypeStruct((b * ra, s), BF16), jax.ShapeDtypeStruct((b * rbc, s), BF16),
                   jax.ShapeDtypeStruct((b * rbc, s), BF16)),
        grid=(n // tm,),
        in_specs=[
            pl.BlockSpec((tm, d), lambda i: (i, 0)),
            pl.BlockSpec((1, d), const),
            _resident((d, ncol), const),
            pl.BlockSpec((1, ncol), const),
            pl.BlockSpec((1, LANE), const),
            _resident((tm, tm), const),
            _resident(pq.shape, const),
            _resident(pk.shape, const),
            _resident((ra, d), const),
            _resident((rbc, d), const),
            _resident((rbc, d), const),
            pl.BlockSpec((ra, 1), const),
            pl.BlockSpec((rbc, 1), const),
        ],
        out_specs=(
            pl.BlockSpec((tm, ncol), lambda i: (i, 0)),
            pl.BlockSpec((ra, tm), vt_map),
            pl.BlockSpec((rbc, tm), vt_map),
            pl.BlockSpec((rbc, tm), vt_map),
        ),
        scratch_shapes=[pltpu.VMEM((1, LANE), F32)],
        compiler_params=_cparams("arbitrary"),
        name="in_proj",
    )(x2, g, w_qk, brow, bf_row, tri, pq, pk, wva, wvb, wvc, onea, onebc)


def _bias_tile_kernel(x_ref, o_ref, *, distance, band):
    d = distance(*(pl.program_id(a) for a in range(len(o_ref.shape) - 2)))
    x = jnp.broadcast_to(x_ref[0, 0], (T, 2 * T))
    tile = pltpu.roll(x, T + 1, 1, stride=1, stride_axis=0)[:, :T] * LOG2E
    ahead = (lax.broadcasted_iota(jnp.int32, (T, T), 0) // CHUNK
             - lax.broadcasted_iota(jnp.int32, (T, T), 1) // CHUNK
             - d * (T // CHUNK))
    if band:
        tile = jnp.where(ahead >= -B_LEFT_CHUNKS, tile, NEG_INF)
    o_ref[...] = jnp.where(ahead <= 0, tile, NEG_INF).reshape(o_ref.shape)


def _bias_tiles(x, lead, distance, d_min, *, band, name):
    heads = x.shape[1]
    zeros = (0,) * 2
    return pl.pallas_call(
        functools.partial(_bias_tile_kernel, distance=distance, band=band),
        out_shape=jax.ShapeDtypeStruct((heads, *lead, T, T), F32),
        grid=(heads, *lead),
        in_specs=[pl.BlockSpec((1, 1, 1, 2 * T),
                               lambda h, *idx: (distance(h, *idx) - d_min, h, 0, 0))],
        out_specs=pl.BlockSpec((1,) * (1 + len(lead)) + (T, T), lambda h, *idx: (h, *idx, *zeros)),
        compiler_params=_cparams(*(("parallel",) * (1 + len(lead)))),
        name=name,
    )(x)


def _t5_distance(h, parity, which, half):
    return parity + 2 * (1 - which) - half


def _band_distance(h, first, half):
    return 1 - half - first


def _softmax_step(s, vt, m_ref, acc_ref, col_max=None):
    rows = vt.shape[0]
    m_prev = m_ref[...]
    if col_max is None:
        col_max = jnp.max(s, axis=0, keepdims=True)
    m_new = jnp.maximum(m_prev, col_max)
    alpha = jnp.exp2(m_prev - m_new)
    p = jnp.exp2(s - m_new).astype(BF16)
    acc_ref[:rows, :] = acc_ref[:rows, :] * alpha + jnp.dot(vt, p, preferred_element_type=F32)
    m_ref[...] = m_new


def _keys(j, n_blocks=1):
    return pl.ds(pl.multiple_of(j * T, T), n_blocks * T)


def _init(m_ref, acc_ref):
    m_ref[...] = jnp.full_like(m_ref, NEG_INF)
    acc_ref[...] = jnp.zeros_like(acc_ref)


def _attn_a_kernel(q_ref, k_ref, vt_ref, tile_ref, lp_ref, sg_ref, o_ref,
                   m0_ref, m1_ref, acc0_ref, acc1_ref, s0a_ref, s0b_ref, s1a_ref, s1b_ref,
                   smax_ref, *, lam_init):
    i = pl.program_id(2)
    streams = ((m0_ref, acc0_ref), (m1_ref, acc1_ref))
    buffers = ((s0a_ref, s0b_ref), (s1a_ref, s1b_ref))
    for m_ref, acc_ref in streams:
        _init(m_ref, acc_ref)
    q = q_ref[...]
    parity = i % 2
    n_steps = i // 2 + 1
    n_far = jnp.maximum(n_steps - 2, 0)

    def fill(which, t, n_blocks=2):
        for sub in range(2):
            lanes = slice(sub * LANE, (sub + 1) * LANE)
            s = _dot_nt(k_ref[_keys(2 * t, n_blocks), lanes], q[:, lanes])
            buffers[sub][which][:n_blocks * T, :] = s
            smax_ref[2 * sub + which] = jnp.max(s, axis=0, keepdims=True)

    def consume(which, t, near=None, par=None):
        n_blocks = 1 if (near == 1 and par == 0) else 2
        vt = vt_ref[:, _keys(2 * t, n_blocks)]
        for sub, (m_ref, acc_ref) in enumerate(streams):
            s = buffers[sub][which][:n_blocks * T, :]
            col_max = smax_ref[2 * sub + which]
            if near is not None:
                s = s + tile_ref[0, par, near, :n_blocks * T, :]
                col_max = None
            _softmax_step(s, vt, m_ref, acc_ref, col_max)

    fill(0, 0)

    def far_body(u, carry):
        fill(1, 2 * u + 1)
        consume(0, 2 * u)
        fill(0, 2 * u + 2)
        consume(1, 2 * u + 1)
        return carry

    lax.fori_loop(0, n_far // 2, far_body, 0)

    for par in range(2):
        @pl.when(jnp.logical_and(n_steps == 1, parity == par))
        def _(par=par):
            consume(0, 0, 1, par)

        @pl.when(jnp.logical_and(jnp.logical_and(n_steps >= 2, n_far % 2 == 0), parity == par))
        def _(par=par):
            fill(1, n_steps - 1, par + 1)
            consume(0, n_steps - 2, 0, par)
            consume(1, n_steps - 1, 1, par)

        @pl.when(jnp.logical_and(n_far % 2 == 1, parity == par))
        def _(par=par):
            fill(1, n_far)
            consume(0, n_far - 1)
            fill(0, n_far + 1, par + 1)
            consume(1, n_far, 0, par)
            consume(0, n_far + 1, 1, par)

    lp = lp_ref[...]
    lam = (jnp.exp(jnp.sum(lp[0:1] * lp[1:2], axis=-1, keepdims=True))
           - jnp.exp(jnp.sum(lp[2:3] * lp[3:4], axis=-1, keepdims=True)) + lam_init)
    acc0 = acc0_ref[...]
    acc1 = acc1_ref[...]
    ot = (acc0[:A_VDIM] / acc0[A_VDIM:A_VDIM + 1]
          - lam * (acc1[:A_VDIM] / acc1[A_VDIM:A_VDIM + 1]))
    ot = _rms(ot, sg_ref[...], axis=0) * (1.0 - lam_init)
    o_ref[...] = ot.T.astype(BF16)


def _attn_a(qk, vt, tiles, lp, sg, *, b, s, lam_init):
    nq = s // T
    wa = 2 * LANE
    return pl.pallas_call(
        functools.partial(_attn_a_kernel, lam_init=lam_init),
        out_shape=jax.ShapeDtypeStruct((b * s, A_HEADS * A_VDIM), BF16),
        grid=(b, A_HEADS, nq),
        in_specs=[
            pl.BlockSpec((T, wa), lambda bi, h, i: (bi * nq + i, h)),
            pl.BlockSpec((s, wa), lambda bi, h, i: (bi, A_HEADS + h)),
            pl.BlockSpec((VA_ROWS, s), lambda bi, h, i: (bi * A_HEADS + h, 0)),
            _resident((1, 2, 2, 2 * T, T), lambda bi, h, i: (h, 0, 0, 0, 0)),
            pl.BlockSpec((4, HEAD_DIM), lambda bi, h, i: (0, 0)),
            pl.BlockSpec((A_VDIM, 1), lambda bi, h, i: (0, 0)),
        ],
        out_specs=pl.BlockSpec((T, A_VDIM), lambda bi, h, i: (bi * nq + i, h)),
        scratch_shapes=[pltpu.VMEM((1, T), F32), pltpu.VMEM((1, T), F32),
                        pltpu.VMEM((VA_ROWS, T), F32), pltpu.VMEM((VA_ROWS, T), F32)]
        + [pltpu.VMEM((2 * T, T), F32)] * 4 + [pltpu.VMEM((4, 1, T), F32)],
        compiler_params=_cparams("parallel", "parallel", "arbitrary"),
        name="attn_a",
    )(qk, qk, vt, tiles, lp, sg)


def _pair_output(acc_refs):
    halves = []
    for acc_ref in acc_refs:
        acc = acc_ref[...]
        halves.append(acc[:HEAD_DIM] / acc[HEAD_DIM:HEAD_DIM + 1])
    return jnp.concatenate(halves, axis=0).T.astype(BF16)


def _pair_values(vt_ref, sub, keys):
    return vt_ref[sub * VBC_ROWS:(sub + 1) * VBC_ROWS, keys]


B_QBLOCKS = 4


def _attn_b_kernel(q_ref, k_ref, vt_ref, tile_ref, o_ref, s0a_ref, s0b_ref, s1a_ref, s1b_ref,
                   smax_ref):
    buffers = ((s0a_ref, s0b_ref), (s1a_ref, s1b_ref))
    lane_head = lax.broadcasted_iota(jnp.int32, (T, LANE), 1) // HEAD_DIM

    def keys_of(r):
        g = pl.program_id(2) * B_QBLOCKS + r
        return g, _keys(jnp.maximum(g - 1, 0), 2)

    def fill(which, r):
        g, keys = keys_of(r)
        first = (g == 0).astype(jnp.int32)
        k = k_ref[keys, :]
        q = q_ref[r * T:(r + 1) * T, :]
        for sub in range(2):
            q_head = jnp.where(lane_head == sub, q, jnp.zeros_like(q))
            s = _dot_nt(k, q_head) + tile_ref[sub, first]
            buffers[sub][which][...] = s
            smax_ref[2 * sub + which] = jnp.max(s, axis=0, keepdims=True)

    def consume(which, r):
        _, keys = keys_of(r)
        halves = []
        for sub in range(2):
            p = jnp.exp2(buffers[sub][which][...] - smax_ref[2 * sub + which]).astype(BF16)
            acc = jnp.dot(_pair_values(vt_ref, sub, keys), p, preferred_element_type=F32)
            halves.append(acc[:HEAD_DIM] / acc[HEAD_DIM:HEAD_DIM + 1])
        o_ref[r * T:(r + 1) * T, :] = jnp.concatenate(halves, axis=0).T.astype(BF16)

    fill(0, 0)
    for r in range(B_QBLOCKS):
        if r + 1 < B_QBLOCKS:
            fill((r + 1) % 2, r + 1)
        consume(r % 2, r)


def _attn_b(qk, vt, tiles, *, b, s, q_block0, k_block0):
    groups = s // (B_QBLOCKS * T)
    pairs = B_HEADS // 2
    return pl.pallas_call(
        _attn_b_kernel,
        out_shape=jax.ShapeDtypeStruct((b * s, B_HEADS * HEAD_DIM), BF16),
        grid=(b, pairs, groups),
        in_specs=[
            pl.BlockSpec((B_QBLOCKS * T, LANE), lambda bi, hp, i: (bi * groups + i, q_block0 + hp)),
            pl.BlockSpec((s, LANE), lambda bi, hp, i: (bi, k_block0 + hp)),
            pl.BlockSpec((2 * VBC_ROWS, s), lambda bi, hp, i: (bi * pairs + hp, 0)),
            pl.BlockSpec((2, 2, 2 * T, T), lambda bi, hp, i: (hp, 0, 0, 0)),
        ],
        out_specs=pl.BlockSpec((B_QBLOCKS * T, 2 * HEAD_DIM),
                               lambda bi, hp, i: (bi * groups + i, hp)),
        scratch_shapes=[pltpu.VMEM((2 * T, T), F32)] * 4 + [pltpu.VMEM((4, 1, T), F32)],
        compiler_params=_cparams("parallel", "parallel", "arbitrary"),
        name="attn_b",
    )(qk, qk, vt, tiles)


def _attn_c_kernel(q_ref, k_ref, vt_ref, o_ref, m0_ref, m1_ref, acc0_ref, acc1_ref,
                   s0a_ref, s0b_ref, s1a_ref, s1b_ref, smax_ref):
    i = pl.program_id(2)
    streams = ((m0_ref, acc0_ref), (m1_ref, acc1_ref))
    buffers = ((s0a_ref, s0b_ref), (s1a_ref, s1b_ref))
    for m_ref, acc_ref in streams:
        _init(m_ref, acc_ref)
    q = q_ref[...]
    n_plain = i // 2

    def fill(which, t, n_blocks=2):
        for sub in range(2):
            lanes = slice(sub * LANE, (sub + 1) * LANE)
            s = _dot_nt(k_ref[_keys(2 * t, n_blocks), lanes], q[:, lanes])
            buffers[sub][which][:n_blocks * T, :] = s
            smax_ref[2 * sub + which] = jnp.max(s, axis=0, keepdims=True)

    def consume(which, t, diagonal_at=None):
        n_blocks = 2 if diagonal_at is None else diagonal_at + 1
        keys = _keys(2 * t, n_blocks)
        if diagonal_at is not None:
            ahead = (lax.broadcasted_iota(jnp.int32, (n_blocks * T, T), 0)
                     - lax.broadcasted_iota(jnp.int32, (n_blocks * T, T), 1))
            causal = ahead <= diagonal_at * T
        for sub, (m_ref, acc_ref) in enumerate(streams):
            s = buffers[sub][which][:n_blocks * T, :]
            col_max = smax_ref[2 * sub + which]
            if diagonal_at is not None:
                s = jnp.where(causal, s, NEG_INF)
                col_max = None
            _softmax_step(s, _pair_values(vt_ref, sub, keys), m_ref, acc_ref, col_max)

    fill(0, 0)

    def body(u, carry):
        fill(1, 2 * u + 1)
        consume(0, 2 * u)
        fill(0, 2 * u + 2)
        consume(1, 2 * u + 1)
        return carry

    lax.fori_loop(0, n_plain // 2, body, 0)

    for diagonal_at in range(2):
        @pl.when(jnp.logical_and(n_plain % 2 == 0, i % 2 == diagonal_at))
        def _(diagonal_at=diagonal_at):
            consume(0, n_plain, diagonal_at)

        @pl.when(jnp.logical_and(n_plain % 2 == 1, i % 2 == diagonal_at))
        def _(diagonal_at=diagonal_at):
            fill(1, n_plain, diagonal_at + 1)
            consume(0, n_plain - 1)
            consume(1, n_plain, diagonal_at)

    o_ref[...] = _pair_output((acc0_ref, acc1_ref))


def _attn_bc(kernel, q_arr, k_arr, vt, extra, extra_specs, *, b, s, heads, width, q_block0,
             k_block0, name, score_buffers=0):
    nq = s // T
    pairs = heads // 2
    return pl.pallas_call(
        kernel,
        out_shape=jax.ShapeDtypeStruct((b * s, heads * HEAD_DIM), BF16),
        grid=(b, pairs, nq),
        in_specs=[
            pl.BlockSpec((T, width), lambda bi, hp, i: (bi * nq + i, q_block0 + hp)),
            pl.BlockSpec((s, width), lambda bi, hp, i: (bi, k_block0 + hp)),
            pl.BlockSpec((2 * VBC_ROWS, s), lambda bi, hp, i: (bi * pairs + hp, 0)),
            *extra_specs,
        ],
        out_specs=pl.BlockSpec((T, 2 * HEAD_DIM), lambda bi, hp, i: (bi * nq + i, hp)),
        scratch_shapes=[pltpu.VMEM((1, T), F32)] * 2 + [pltpu.VMEM((VBC_ROWS, T), F32)] * 2
        + [pltpu.VMEM((2 * T, T), F32)] * score_buffers
        + [pltpu.VMEM((score_buffers, 1, T), F32)] * (score_buffers > 0),
        compiler_params=_cparams("parallel", "parallel", "arbitrary"),
        name=name,
    )(q_arr, k_arr, vt, *extra)


def _merge_kernel(x_ref, g_ref, wg_ref, oa_ref, ob_ref, oc_ref, wa_ref, wb_ref, wc_ref,
                  wo_ref, o_ref):
    x = x_ref[...]
    d = x.shape[1]
    h = _rms(x, g_ref[...]).astype(BF16)
    gates = jax.nn.sigmoid(jnp.dot(h, wg_ref[...], preferred_element_type=F32))
    merged = None
    for n, (o_br, w_br) in enumerate(((oa_ref, wa_ref), (ob_ref, wb_ref), (oc_ref, wc_ref))):
        term = gates[:, n * d:(n + 1) * d] * jnp.dot(o_br[...], w_br[...],
                                                     preferred_element_type=F32)
        merged = term if merged is None else merged + term
    o_ref[...] = x + jnp.dot(merged.astype(BF16), wo_ref[...], preferred_element_type=F32)


def _merge(x2, g, wg, oa, ob, oc, wa, wb, wc, wo, *, tm):
    n, d = x2.shape
    row = lambda i: (i, 0)
    const = lambda i: (0, 0)
    return pl.pallas_call(
        _merge_kernel,
        out_shape=jax.ShapeDtypeStruct((n, d), F32),
        grid=(n // tm,),
        in_specs=[
            pl.BlockSpec((tm, d), row),
            pl.BlockSpec((1, d), const),
            _resident(wg.shape, const),
            pl.BlockSpec((tm, oa.shape[1]), row),
            pl.BlockSpec((tm, ob.shape[1]), row),
            pl.BlockSpec((tm, oc.shape[1]), row),
            _resident(wa.shape, const),
            _resident(wb.shape, const),
            _resident(wc.shape, const),
            _resident(wo.shape, const),
        ],
        out_specs=pl.BlockSpec((tm, d), row),
        compiler_params=_cparams("parallel"),
        name="merge",
    )(x2, g, wg, oa, ob, oc, wa, wb, wc, wo)


def _ffn_kernel(x_ref, g_ref, wg_ref, wu_ref, wd_ref, gf_ref, o_ref, *, n_chunks, final_norm):
    x = x_ref[...]
    h = _rms(x, g_ref[...]).astype(BF16)
    fc = wg_ref.shape[1] // n_chunks
    y = x
    for c in range(n_chunks):
        cols = slice(c * fc, (c + 1) * fc)
        gt = jnp.dot(h, wg_ref[:, cols], preferred_element_type=F32)
        up = jnp.dot(h, wu_ref[:, cols], preferred_element_type=F32)
        act = (gt * jax.nn.sigmoid(gt) * up).astype(BF16)
        y = y + jnp.dot(act, wd_ref[cols, :], preferred_element_type=F32)
    if final_norm:
        y = _rms(y, gf_ref[...])
    o_ref[...] = y


def _ffn(x2, g, wg, wu, wd, gf, *, tm, n_chunks, final_norm):
    n, d = x2.shape
    row = lambda i: (i, 0)
    const = lambda i: (0, 0)
    return pl.pallas_call(
        functools.partial(_ffn_kernel, n_chunks=n_chunks, final_norm=final_norm),
        out_shape=jax.ShapeDtypeStruct((n, d), F32),
        grid=(n // tm,),
        in_specs=[
            pl.BlockSpec((tm, d), row),
            pl.BlockSpec((1, d), const),
            _resident(wg.shape, const),
            _resident(wu.shape, const),
            _resident(wd.shape, const),
            pl.BlockSpec((1, d), const),
        ],
        out_specs=pl.BlockSpec((tm, d), row),
        compiler_params=_cparams("parallel"),
        name="ffn",
    )(x2, g, wg, wu, wd, gf)


def _pad_heads(w, heads, width, slot):
    d = w.shape[0]
    w = w.reshape(d, heads, width)
    return jnp.pad(w, ((0, 0), (0, 0), (0, slot - width))).reshape(d, heads * slot)


def _ones_col(heads, slot, at):
    col = np.zeros((heads, slot), np.float32)
    col[:, at] = 1.0
    return jnp.asarray(col.reshape(heads * slot, 1))


def _t5_bucket(rel):
    nb = T5_BUCKETS // 2
    max_exact = nb // 2
    n = jnp.abs(rel)
    nf = jnp.maximum(n, 1).astype(jnp.float32)
    large = max_exact + (jnp.log(nf / max_exact) / math.log(T5_MAX_DIST / max_exact)
                         * (nb - max_exact)).astype(jnp.int32)
    large = jnp.minimum(large, nb - 1)
    return jnp.where(rel > 0, nb, 0) + jnp.where(n < max_exact, n, large)


def _tile_rel(d_min, d_max):
    return ((T - 1 - jnp.arange(2 * T))[None, :]
            - jnp.arange(d_min, d_max + 1)[:, None] * T)


def _rel_vectors(table, idx):
    return jnp.moveaxis(table[idx], -1, 1).astype(F32)[:, :, None, :]


def kernel(x, norm_mix_g, w_in, b_forget, diff_lambda, diff_subln_g, t5_table, b_rel_table,
           w_br_a, w_br_b, w_br_c, w_out, norm_ffn_g, w_gate_up, w_down, final_norm_g):
    b, s, d = x.shape
    depth = w_in.shape[0]
    assert s % (B_QBLOCKS * T) == 0 and B_QBLOCKS % 2 == 0 and T // CHUNK == B_LEFT_CHUNKS
    n = b * s
    scale = HEAD_DIM ** -0.5 * LOG2E
    a_qk_w = A_HEADS * 2 * HEAD_DIM
    a_v_w = A_HEADS * A_VDIM
    bw = B_HEADS * HEAD_DIM
    cw = C_HEADS * HEAD_DIM
    d_ff = w_down.shape[1]
    x2 = x.reshape(n, d)

    t5_far = t5_table[T5_BUCKETS // 2 - 1].astype(F32)
    t5_tiles = _bias_tiles(
        _rel_vectors(t5_table, _t5_bucket(_tile_rel(-1, A_NEAR))) - t5_far[None, :, None, None],
        (2, 2, 2), _t5_distance, -1, band=False, name="bias_tiles_t5",
    ).reshape(A_HEADS, 2, 2, 2 * T, T)
    far_pieces, rest = [], t5_far * LOG2E
    for _ in range(C_PARTS):
        piece = rest.astype(BF16).astype(F32)
        far_pieces.append(piece)
        rest = rest - piece
    far_pieces = jnp.repeat(jnp.stack(far_pieces, axis=-1), 2, axis=0)
    spare = slice(HEAD_DIM, HEAD_DIM + C_PARTS)
    brow_qa = jnp.zeros((2 * A_HEADS, LANE), F32).at[:, spare].set(1.0)
    brow_ka = jnp.zeros((2 * A_HEADS, LANE), F32).at[:, spare].set(far_pieces)
    brow_qc = jnp.zeros((C_HEADS, LANE), F32).at[:, HEAD_DIM + C_PARTS:HEAD_DIM + 2 * C_PARTS].set(1.0)
    brow_kc = jnp.zeros((C_HEADS, LANE), F32).at[:, spare].set(1.0)
    b_slot0 = 4 * A_HEADS
    c_slot0 = b_slot0 + B_HEADS
    brow = jnp.concatenate([brow_qa.reshape(1, -1), brow_ka.reshape(1, -1),
                            jnp.zeros((1, 2 * B_HEADS * HEAD_DIM), F32),
                            brow_qc.reshape(1, -1), brow_kc.reshape(1, -1)], axis=1)
    band_idx = jnp.clip(_tile_rel(-1, 1), -B_REL_CLIP, CHUNK - 1) + B_REL_CLIP
    onea = _ones_col(A_HEADS, VA_ROWS, A_VDIM)
    onebc = _ones_col(B_HEADS, VBC_ROWS, HEAD_DIM)

    for l in range(depth):
        w = w_in[l]
        pieces, at = [], 0
        for width in (a_qk_w, a_qk_w, a_v_w, bw, bw, bw, cw, cw, cw, C_HEADS, N_BRANCH * d):
            pieces.append(w[:, at:at + width])
            at += width
        a_q, a_k, a_v, b_q, b_k, b_v, c_q, c_k, c_v, c_f, w_gates = pieces
        w_qk = jnp.concatenate([
            _pad_heads(a_q * scale, 2 * A_HEADS, HEAD_DIM, LANE),
            _pad_heads(a_k, 2 * A_HEADS, HEAD_DIM, LANE),
            b_q * scale,
            b_k,
            _pad_heads(c_q * scale, C_HEADS, HEAD_DIM, LANE),
            _pad_heads(c_k, C_HEADS, HEAD_DIM, LANE),
        ], axis=1).at[:, F_LANE0:F_LANE0 + C_HEADS].set(c_f).astype(BF16)
        wva = _pad_heads(a_v, A_HEADS, A_VDIM, VA_ROWS).T.astype(BF16)
        wvb = _pad_heads(b_v, B_HEADS, HEAD_DIM, VBC_ROWS).T.astype(BF16)
        wvc = _pad_heads(c_v, C_HEADS, HEAD_DIM, VBC_ROWS).T.astype(BF16)

        bf_row = jnp.zeros((1, LANE), F32).at[0, F_LANE0:F_LANE0 + C_HEADS].set(b_forget[l])
        qk, vta, vtb, vtc = _in_proj(
            x2, norm_mix_g[l][None], w_qk, brow, bf_row, wva, wvb, wvc, onea, onebc,
            b=b, tm=min(512, s), cq_tile=c_slot0 // C_HEADS)

        lam_init = 0.8 - 0.6 * math.exp(-0.3 * l)
        o_a = _attn_a(qk, vta, t5_tiles, diff_lambda[l].astype(F32),
                      diff_subln_g[l][:, None].astype(F32), b=b, s=s, lam_init=lam_init)

        band_tiles = _bias_tiles(
            _rel_vectors(b_rel_table[l], band_idx), (2, 2), _band_distance, -1, band=True,
            name="bias_tiles_band").reshape(B_HEADS, 2, 2 * T, T)
        o_b = _attn_b(qk, vtb, band_tiles, b=b, s=s, q_block0=b_slot0,
                      k_block0=b_slot0 + B_HEADS // 2)

        o_c = _attn_bc(_attn_c_kernel, qk, qk, vtc, (), (), b=b, s=s, heads=C_HEADS,
                       width=2 * LANE, q_block0=c_slot0 // 2, k_block0=(c_slot0 + C_HEADS) // 2,
                       name="attn_c", score_buffers=4)

        x2 = _merge(
            x2, norm_mix_g[l][None], w_gates.astype(BF16), o_a, o_b, o_c,
            w_br_a[l].astype(BF16), w_br_b[l].astype(BF16), w_br_c[l].astype(BF16),
            w_out[l].astype(BF16), tm=min(512, n))

        x2 = _ffn(
            x2, norm_ffn_g[l][None], w_gate_up[l][:, :d_ff].astype(BF16),
            w_gate_up[l][:, d_ff:].astype(BF16), w_down[l].astype(BF16),
            final_norm_g[None], tm=min(512, n), n_chunks=2, final_norm=(l == depth - 1))

    return x2.reshape(b, s, d)
```
